```python
import jax, jax.numpy as jnp
from jax import lax
import numpy as np

D_MODEL = 1024
BATCH = 8
SEQ = 2048
DEPTH = 4
DEC_BATCH = 128
DEC_SEQ = 1
PAST_LEN = 2048
PAGE_SIZE = 128

HEAD_DIM = 64
POOL_WINDOWS = (2, 4, 8, 16)
POOL_GROUP = 64
POOL_WIDTH = len(POOL_WINDOWS) * POOL_GROUP
POOL_HIST = max(POOL_WINDOWS) - 1
FOX_HEADS = 8
FOX_WIDTH = FOX_HEADS * HEAD_DIM
MEM_HEADS = 4
MEM_WIDTH = MEM_HEADS * HEAD_DIM
MEM_TOKENS = 256
MIX_WIDTH = POOL_WIDTH + FOX_WIDTH + MEM_WIDTH
IN_COLS = POOL_WIDTH + 3 * FOX_WIDTH + FOX_HEADS + MEM_WIDTH
D_FF = 2816
Q_BLOCK = 128
EPS = 1e-6

kernel_name = "hymba_pool_fox_memory_macaron"


def rmsnorm(x, g):
    xf = x.astype(jnp.float32)
    y = xf * lax.rsqrt(jnp.mean(xf * xf, axis=-1, keepdims=True) + EPS)
    return (y * g.astype(jnp.float32)).astype(x.dtype)


def half_ffn(x, g, w_gu, w_down):
    gate, up = jnp.split(rmsnorm(x, g) @ w_gu, 2, axis=-1)
    return x + 0.5 * ((jax.nn.silu(gate) * up) @ w_down)


def split_mix_input(u, b_f, g_q, g_k, g_mq):
    B, L, _ = u.shape
    o1 = POOL_WIDTH
    o2 = o1 + FOX_WIDTH
    o3 = o2 + FOX_WIDTH
    o4 = o3 + FOX_WIDTH
    o5 = o4 + FOX_HEADS
    u_pool = u[..., :o1]
    q = rmsnorm(u[..., o1:o2].reshape(B, L, FOX_HEADS, HEAD_DIM), g_q)
    k = rmsnorm(u[..., o2:o3].reshape(B, L, FOX_HEADS, HEAD_DIM), g_k)
    v = u[..., o3:o4].reshape(B, L, FOX_HEADS, HEAD_DIM)
    logf = jax.nn.log_sigmoid((u[..., o4:o5] + b_f).astype(jnp.float32))
    mq = rmsnorm(u[..., o5:].reshape(B, L, MEM_HEADS, HEAD_DIM), g_mq)
    return u_pool, q, k, v, logf, mq


def pool_mix(z, pos0, n_hist, w_pool, scale):
    B, T, _ = z.shape
    pos = pos0 + jnp.arange(T)
    zf = z.astype(jnp.float32).reshape(B, T, len(POOL_WINDOWS), POOL_GROUP)
    outs = []
    for g, w in enumerate(POOL_WINDOWS):
        zg = zf[:, :, g]
        cs = jnp.cumsum(jnp.pad(zg, ((0, 0), (w, 0), (0, 0))), axis=1)
        cnt = jnp.minimum(pos + 1, w).astype(jnp.float32)
        outs.append((cs[:, w:] - cs[:, :-w]) / cnt[None, :, None] - zg)
    pooled = jnp.stack(outs, axis=2)[:, n_hist:].astype(z.dtype)
    y = jnp.einsum('blgc,gcd->blgd', pooled, w_pool)
    return y.reshape(B, T - n_hist, POOL_WIDTH) * scale


def fox_attend(q, k, v, cq, ck, qpos, kpos):
    s = jnp.einsum('bqhd,bkhd->bhqk', q, k).astype(jnp.float32) * (HEAD_DIM ** -0.5)
    s = s + (jnp.transpose(cq, (0, 2, 1))[..., :, None] - jnp.transpose(ck, (0, 2, 1))[..., None, :])
    s = jnp.where(kpos[None, :] <= qpos[:, None], s, -jnp.inf)
    p = jax.nn.softmax(s, axis=-1).astype(v.dtype)
    return jnp.einsum('bhqk,bkhd->bqhd', p, v)


def fox_prompt(q, k, v, c):
    B, S, H, D = q.shape
    kpos = jnp.arange(S)

    def block(i):
        st = i * Q_BLOCK
        qb = lax.dynamic_slice_in_dim(q, st, Q_BLOCK, axis=1)
        cb = lax.dynamic_slice_in_dim(c, st, Q_BLOCK, axis=1)
        return fox_attend(qb, k, v, cb, c, st + jnp.arange(Q_BLOCK), kpos)

    o = lax.map(block, jnp.arange(S // Q_BLOCK))
    return jnp.moveaxis(o, 0, 1).reshape(B, S, H, D)


def mem_kv(mem, g, w_kv, g_k):
    B, M, _ = mem.shape
    mk, mv = jnp.split(rmsnorm(mem, g) @ w_kv, 2, axis=-1)
    mk = rmsnorm(mk.reshape(B, M, MEM_HEADS, HEAD_DIM), g_k)
    return mk, mv.reshape(B, M, MEM_HEADS, HEAD_DIM)


def mem_attend(mq, mk, mv):
    s = jnp.einsum('blhd,bmhd->bhlm', mq, mk).astype(jnp.float32) * (HEAD_DIM ** -0.5)
    p = jax.nn.softmax(s, axis=-1).astype(mv.dtype)
    return jnp.einsum('bhlm,bmhd->blhd', p, mv)


def mix_out(o_pool, o_fox, o_mem, w_out):
    B, L, _ = o_pool.shape
    cat = jnp.concatenate([o_pool, o_fox.reshape(B, L, FOX_WIDTH), o_mem.reshape(B, L, MEM_WIDTH)], axis=-1)
    return cat @ w_out


def setup_inputs(seed: int = 0) -> dict:
    key = jax.random.key(seed)
    keys = iter(jax.random.split(key, 40))

    def nrm(shape, scale=1.0):
        return scale * jax.random.normal(next(keys), shape, jnp.float32)

    def gain(shape):
        return 1.0 + nrm(shape, 0.02)

    n_pages = PAST_LEN // PAGE_SIZE
    used = DEC_BATCH * n_pages
    n_phys = used + max(1, used // 4)
    page_table = jax.random.permutation(next(keys), n_phys)[:used].reshape(DEC_BATCH, n_pages).astype(jnp.int32)

    return {
        'x_prompt': nrm((BATCH, SEQ, D_MODEL)),
        'x_sample': nrm((DEC_BATCH, DEC_SEQ, D_MODEL)),
        'cache_k': nrm((n_phys, DEPTH, PAGE_SIZE, FOX_HEADS, HEAD_DIM)),
        'cache_v': nrm((n_phys, DEPTH, PAGE_SIZE, FOX_HEADS, HEAD_DIM)),
        'cache_logf': jax.nn.log_sigmoid(2.0 + nrm((n_phys, DEPTH, PAGE_SIZE, FOX_HEADS))),
        'state_pool': nrm((DEC_BATCH, DEPTH, POOL_HIST, POOL_WIDTH)),
        'cache_mem_k': nrm((DEC_BATCH, DEPTH, MEM_TOKENS, MEM_HEADS, HEAD_DIM)),
        'cache_mem_v': nrm((DEC_BATCH, DEPTH, MEM_TOKENS, MEM_HEADS, HEAD_DIM)),
        'page_table': page_table,
        'mem_prompt': nrm((BATCH, MEM_TOKENS, D_MODEL)),
        'g_ffn1': gain((DEPTH, D_MODEL)),
        'w_ffn1_gu': nrm((DEPTH, D_MODEL, 2 * D_FF), D_MODEL ** -0.5),
        'w_ffn1_down': nrm((DEPTH, D_FF, D_MODEL), D_FF ** -0.5),
        'g_mix': gain((DEPTH, D_MODEL)),
        'w_in': nrm((DEPTH, D_MODEL, IN_COLS), D_MODEL ** -0.5),
        'b_forget': 2.0 + nrm((DEPTH, FOX_HEADS), 0.5),
        'g_fox_q': gain((DEPTH, HEAD_DIM)),
        'g_fox_k': gain((DEPTH, HEAD_DIM)),
        'w_pool': nrm((DEPTH, len(POOL_WINDOWS), POOL_GROUP, POOL_GROUP), POOL_GROUP ** -0.5),
        'pool_scale': 1.0 + nrm((DEPTH, POOL_WIDTH), 0.1),
        'g_mem': gain((DEPTH, D_MODEL)),
        'w_mem_kv': nrm((DEPTH, D_MODEL, 2 * MEM_WIDTH), D_MODEL ** -0.5),
        'g_mem_q': gain((DEPTH, HEAD_DIM)),
        'g_mem_k': gain((DEPTH, HEAD_DIM)),
        'w_out': nrm((DEPTH, MIX_WIDTH, D_MODEL), MIX_WIDTH ** -0.5),
        'g_ffn2': gain((DEPTH, D_MODEL)),
        'w_ffn2_gu': nrm((DEPTH, D_MODEL, 2 * D_FF), D_MODEL ** -0.5),
        'w_ffn2_down': nrm((DEPTH, D_FF, D_MODEL), D_FF ** -0.5),
    }


def reference(x_prompt, x_sample, cache_k, cache_v, cache_logf, state_pool, cache_mem_k, cache_mem_v,
              page_table, mem_prompt, g_ffn1, w_ffn1_gu, w_ffn1_down, g_mix, w_in, b_forget,
              g_fox_q, g_fox_k, w_pool, pool_scale, g_mem, w_mem_kv, g_mem_q, g_mem_k, w_out,
              g_ffn2, w_ffn2_gu, w_ffn2_down):
    dec_b, dec_l = x_sample.shape[0], x_sample.shape[1]
    past = page_table.shape[1] * cache_k.shape[2]
    qpos_s = past + jnp.arange(dec_l)
    kpos_s = jnp.arange(past + dec_l)
    xp, xs = x_prompt, x_sample
    kp_l, vp_l, fp_l, pp_l, mkp_l, mvp_l = [], [], [], [], [], []
    ks_l, vs_l, fs_l, ps_l = [], [], [], []
    for l in range(DEPTH):
        xp = half_ffn(xp, g_ffn1[l], w_ffn1_gu[l], w_ffn1_down[l])
        u_pool, q, k, v, logf, mq = split_mix_input(rmsnorm(xp, g_mix[l]) @ w_in[l], b_forget[l],
                                                     g_fox_q[l], g_fox_k[l], g_mem_q[l])
        o_pool = pool_mix(u_pool, 0, 0, w_pool[l], pool_scale[l])
        o_fox = fox_prompt(q, k, v, jnp.cumsum(logf, axis=1))
        mk, mv = mem_kv(mem_prompt, g_mem[l], w_mem_kv[l], g_mem_k[l])
        o_mem = mem_attend(mq, mk, mv)
        xp = xp + mix_out(o_pool, o_fox, o_mem, w_out[l])
        xp = half_ffn(xp, g_ffn2[l], w_ffn2_gu[l], w_ffn2_down[l])
        kp_l.append(k)
        vp_l.append(v)
        fp_l.append(logf)
        pp_l.append(u_pool[:, -POOL_HIST:])
        mkp_l.append(mk)
        mvp_l.append(mv)

        xs = half_ffn(xs, g_ffn1[l], w_ffn1_gu[l], w_ffn1_down[l])
        u_pool_s, qs, k_s, v_s, logf_s, mqs = split_mix_input(rmsnorm(xs, g_mix[l]) @ w_in[l], b_forget[l],
                                                               g_fox_q[l], g_fox_k[l], g_mem_q[l])
        z = jnp.concatenate([state_pool[:, l].astype(u_pool_s.dtype), u_pool_s], axis=1)
        o_pool_s = pool_mix(z, past - POOL_HIST, POOL_HIST, w_pool[l], pool_scale[l])
        k_past = cache_k[page_table, l].reshape(dec_b, past, FOX_HEADS, HEAD_DIM)
        v_past = cache_v[page_table, l].reshape(dec_b, past, FOX_HEADS, HEAD_DIM)
        f_past = cache_logf[page_table, l].reshape(dec_b, past, FOX_HEADS).astype(jnp.float32)
        k_all = jnp.concatenate([k_past.astype(k_s.dtype), k_s], axis=1)
        v_all = jnp.concatenate([v_past.astype(v_s.dtype), v_s], axis=1)
        c_all = jnp.cumsum(jnp.concatenate([f_past, logf_s], axis=1), axis=1)
        o_fox_s = fox_attend(qs, k_all, v_all, c_all[:, past:], c_all, qpos_s, kpos_s)
        o_mem_s = mem_attend(mqs, cache_mem_k[:, l].astype(mqs.dtype), cache_mem_v[:, l].astype(mqs.dtype))
        xs = xs + mix_out(o_pool_s, o_fox_s, o_mem_s, w_out[l])
        xs = half_ffn(xs, g_ffn2[l], w_ffn2_gu[l], w_ffn2_down[l])
        ks_l.append(k_s)
        vs_l.append(v_s)
        fs_l.append(logf_s)
        ps_l.append(z[:, -POOL_HIST:])

    k_prompt = jnp.stack(kp_l, axis=1)
    v_prompt = jnp.stack(vp_l, axis=1)
    logf_prompt = jnp.stack(fp_l, axis=1)
    pool_prompt = jnp.stack(pp_l, axis=1)
    mem_k_prompt = jnp.stack(mkp_l, axis=1)
    mem_v_prompt = jnp.stack(mvp_l, axis=1)
    k_sample = jnp.stack(ks_l, axis=1)
    v_sample = jnp.stack(vs_l, axis=1)
    logf_sample = jnp.stack(fs_l, axis=1)
    pool_sample = jnp.stack(ps_l, axis=1)
    return (xp, xs, k_prompt, v_prompt, logf_prompt, pool_prompt, mem_k_prompt, mem_v_prompt,
            k_sample, v_sample, logf_sample, pool_sample)
```

```python
import functools

import jax
import jax.numpy as jnp
from jax import lax
from jax.experimental import pallas as pl
from jax.experimental.pallas import tpu as pltpu

F32 = jnp.float32
BF16 = jnp.bfloat16

HEAD_DIM = 64
POOL_WINDOWS = (2, 4, 8, 16)
POOL_GROUP = 64
POOL_WIDTH = len(POOL_WINDOWS) * POOL_GROUP
POOL_HIST = max(POOL_WINDOWS) - 1
EPS = 1e-6
SCALE = HEAD_DIM ** -0.5

LANES = 128
SUBLANES = 8
VMEM_LIMIT_BYTES = 56 * 1024 * 1024
NEG_BIG = -1e30

TOKEN_TILE = 512
FFN_CHUNK = 256
ATTN_BLOCK = 256
HALO = 16


def _dot(a, b):
    return jnp.dot(a, b, preferred_element_type=F32)


def _dot_nt(a, b):
    return lax.dot_general(a, b, (((1,), (1,)), ((), ())), preferred_element_type=F32)


def _rms(x, g):
    return x * lax.rsqrt(jnp.mean(x * x, axis=-1, keepdims=True) + EPS) * g


def _params(n_axes):
    return pltpu.CompilerParams(dimension_semantics=("arbitrary",) * n_axes, vmem_limit_bytes=VMEM_LIMIT_BYTES)


def _resident(shape, index_map):
    return pl.BlockSpec(shape, index_map, pipeline_mode=pl.Buffered(1))


def _ffn_body(x_ref, g_ref, wgu_ref, wd_ref, o_ref, *, d_ff):
    x = x_ref[...]
    xn = _rms(x, g_ref[...]).astype(BF16)
    acc = None
    for c in range(d_ff // FFN_CHUNK):
        lo = c * FFN_CHUNK
        gate = _dot(xn, wgu_ref[:, lo:lo + FFN_CHUNK])
        up = _dot(xn, wgu_ref[:, d_ff + lo:d_ff + lo + FFN_CHUNK])
        act = (gate * jax.nn.sigmoid(gate) * up).astype(BF16)
        part = _dot(act, wd_ref[lo:lo + FFN_CHUNK, :])
        acc = part if acc is None else acc + part
    o_ref[...] = x + 0.5 * acc


def _ffn(x, g, wgu, wd, layer, tm):
    t, d = x.shape
    d_ff = wd.shape[1]
    return pl.pallas_call(
        functools.partial(_ffn_body, d_ff=d_ff),
        grid=(t // tm,),
        in_specs=[
            pl.BlockSpec((tm, d), lambda i: (i, 0)),
            pl.BlockSpec((None, 1, d), lambda i: (layer, 0, 0)),
            _resident((None, d, 2 * d_ff), lambda i: (layer, 0, 0)),
            _resident((None, d_ff, d), lambda i: (layer, 0, 0)),
        ],
        out_specs=pl.BlockSpec((tm, d), lambda i: (i, 0)),
        out_shape=jax.ShapeDtypeStruct((t, d), F32),
        compiler_params=_params(1),
        name="ffn",
    )(x, g, wgu, wd)


def _head_norm(u, g, bd_ref):
    sq = (u * u).astype(BF16)
    ms = jnp.concatenate([_dot(sq[:, o:o + 256], bd_ref[...]) for o in range(0, u.shape[1], 256)], axis=1)
    return u * lax.rsqrt(ms + EPS) * g


def _proj_common(x_ref, g_ref, w_ref, bf_ref, gq_ref, gk_ref, gmq_ref, bd_ref):
    xn = _rms(x_ref[...], g_ref[...]).astype(BF16)
    u = _dot(xn, w_ref[...])
    u_pool = u[:, 0:256]
    q = _head_norm(u[:, 256:768], gq_ref[...], bd_ref) * SCALE
    k = _head_norm(u[:, 768:1280], gk_ref[...], bd_ref)
    v = u[:, 1280:1792]
    mq = _head_norm(u[:, 1792:2048], gmq_ref[...], bd_ref) * SCALE
    logf = jax.nn.log_sigmoid(u[:, 2048:2176] + bf_ref[...])
    return u_pool, q, k, v, mq, logf


def _proj_prompt_body(x_ref, g_ref, w_ref, bf_ref, gq_ref, gk_ref, gmq_ref, bd_ref,
                      up_ref, q_ref, k_ref, kb_ref, v_ref, vb_ref, lf_ref, mq_ref, c_ref, ct_ref,
                      carry_ref, *, tm, tiles_per_seq, n_heads):
    u_pool, q, k, v, mq, logf = _proj_common(x_ref, g_ref, w_ref, bf_ref, gq_ref, gk_ref, gmq_ref, bd_ref)
    up_ref[...] = u_pool
    q_ref[...] = q.astype(BF16)
    k_ref[...] = k
    kb_ref[...] = k.astype(BF16)
    v_ref[...] = v
    vb_ref[...] = v.astype(BF16)
    mq_ref[...] = mq.astype(BF16)
    lf_ref[...] = logf[:, :n_heads]

    @pl.when(pl.program_id(0) % tiles_per_seq == 0)
    def _():
        carry_ref[...] = jnp.zeros_like(carry_ref)

    lane_f = lax.broadcasted_iota(jnp.int32, logf.shape, 1)
    ft = jnp.where(lane_f < n_heads, logf, 0.0).T
    lane = lax.broadcasted_iota(jnp.int32, ft.shape, 1)
    shift = 1
    while shift < tm:
        ft = ft + jnp.where(lane >= shift, pltpu.roll(ft, shift, axis=1), 0.0)
        shift *= 2
    ct = ft + jnp.concatenate([carry_ref[...]] * (tm // LANES), axis=1)
    carry_ref[...] = jnp.broadcast_to(ct[:, tm - 1:tm], carry_ref.shape)
    ct_ref[...] = ct[:n_heads]
    c_ref[...] = ct.T[:, :n_heads]


def _proj_sample_body(x_ref, g_ref, w_ref, bf_ref, gq_ref, gk_ref, gmq_ref, bd_ref,
                      up_ref, q_ref, k_ref, v_ref, lf_ref, mq_ref, *, n_heads):
    u_pool, q, k, v, mq, logf = _proj_common(x_ref, g_ref, w_ref, bf_ref, gq_ref, gk_ref, gmq_ref, bd_ref)
    up_ref[...] = u_pool
    q_ref[...] = q
    k_ref[...] = k
    v_ref[...] = v
    mq_ref[...] = mq
    lf_ref[...] = logf[:, :n_heads]


def _proj_in_specs(tm, d, w_cols, layer):
    return [
        pl.BlockSpec((tm, d), lambda i: (i, 0)),
        pl.BlockSpec((None, 1, d), lambda i: (layer, 0, 0)),
        _resident((None, d, w_cols), lambda i: (layer, 0, 0)),
        pl.BlockSpec((None, 1, LANES), lambda i: (layer, 0, 0)),
        pl.BlockSpec((None, 1, 512), lambda i: (layer, 0, 0)),
        pl.BlockSpec((None, 1, 512), lambda i: (layer, 0, 0)),
        pl.BlockSpec((None, 1, 256), lambda i: (layer, 0, 0)),
        _resident((256, 256), lambda i: (0, 0)),
    ]


def _proj_prompt(x, wts, layer, n_seq, seq, n_heads):
    t, d = x.shape
    tm = TOKEN_TILE
    tps = seq // tm
    row = lambda w, dt: (pl.BlockSpec((tm, w), lambda i: (i, 0)), jax.ShapeDtypeStruct((t, w), dt))
    outs = [row(256, F32), row(512, BF16), row(512, F32), row(512, BF16), row(512, F32), row(512, BF16),
            row(n_heads, F32), row(256, BF16), row(n_heads, F32),
            (pl.BlockSpec((None, n_heads, tm), lambda i: (i // tps, 0, i % tps)),
             jax.ShapeDtypeStruct((n_seq, n_heads, seq), F32))]
    return pl.pallas_call(
        functools.partial(_proj_prompt_body, tm=tm, tiles_per_seq=tps, n_heads=n_heads),
        grid=(t // tm,),
        in_specs=_proj_in_specs(tm, d, wts["w_in"].shape[2], layer),
        out_specs=[o[0] for o in outs],
        out_shape=[o[1] for o in outs],
        scratch_shapes=[pltpu.VMEM((LANES, LANES), F32)],
        compiler_params=_params(1),
        name="proj_prompt",
    )(x, wts["g_mix"], wts["w_in"], wts["b_f"], wts["g_q"], wts["g_k"], wts["g_mq"], wts["bd"])


def _proj_sample(x, wts, layer, n_heads):
    t, d = x.shape
    row = lambda w: (pl.BlockSpec((t, w), lambda i: (0, 0)), jax.ShapeDtypeStruct((t, w), F32))
    outs = [row(256), row(512), row(512), row(512), row(n_heads), row(256)]
    return pl.pallas_call(
        functools.partial(_proj_sample_body, n_heads=n_heads),
        grid=(1,),
        in_specs=_proj_in_specs(t, d, wts["w_in"].shape[2], layer),
        out_specs=[o[0] for o in outs],
        out_shape=[o[1] for o in outs],
        compiler_params=_params(1),
        name="proj_sample",
    )(x, wts["g_mix"], wts["w_in"], wts["b_f"], wts["g_q"], wts["g_k"], wts["g_mq"], wts["bd"])


def _fox_body(q_ref, k_ref, v_ref, c_ref, ct_ref, o_ref, m_ref, l_ref, acc_ref, *, blk, n_blk):
    pair = pl.program_id(1)
    lane = lax.broadcasted_iota(jnp.int32, (blk, LANES), 1)
    lane_c = lax.broadcasted_iota(jnp.int32, (blk, c_ref.shape[1]), 1)
    causal = lax.broadcasted_iota(jnp.int32, (blk, blk), 0) >= lax.broadcasted_iota(jnp.int32, (blk, blk), 1)

    def q_block(i, carry):
        q0 = pl.multiple_of(i * blk, blk)
        q2 = q_ref[pl.ds(q0, blk), :]
        zero = jnp.zeros_like(q2)
        q_head = (jnp.where(lane < HEAD_DIM, q2, zero), jnp.where(lane >= HEAD_DIM, q2, zero))
        cq = c_ref[pl.ds(q0, blk), :]
        cq_head = [jnp.sum(jnp.where(lane_c == 2 * pair + e, cq, 0.0), axis=1, keepdims=True) for e in range(2)]
        m_ref[...] = jnp.full_like(m_ref, NEG_BIG)
        l_ref[...] = jnp.zeros_like(l_ref)
        acc_ref[...] = jnp.zeros_like(acc_ref)

        def kv_block(j, masked):
            k0 = pl.multiple_of(j * blk, blk)
            kj = k_ref[pl.ds(k0, blk), :]
            vj = v_ref[pl.ds(k0, blk), :]
            for e in range(2):
                ck = ct_ref[pl.ds(2 * pair + e, 1), pl.ds(k0, blk)]
                s = _dot_nt(q_head[e], kj) + (cq_head[e] - ck)
                if masked:
                    s = jnp.where(causal, s, NEG_BIG)
                m_old = m_ref[e]
                m_new = jnp.maximum(m_old, jnp.max(s, axis=1, keepdims=True))
                alpha = jnp.exp(m_old - m_new)
                p = jnp.exp(s - m_new)
                l_ref[e] = alpha * l_ref[e] + jnp.sum(p, axis=1, keepdims=True)
                acc_ref[e] = alpha * acc_ref[e] + _dot(p.astype(BF16), vj)
                m_ref[e] = m_new

        def full_block(j, c):
            kv_block(j, False)
            return c

        lax.fori_loop(0, i, full_block, 0)
        kv_block(i, True)
        o = jnp.where(lane < HEAD_DIM, acc_ref[0] / l_ref[0], acc_ref[1] / l_ref[1])
        o_ref[pl.ds(q0, blk), :] = o.astype(o_ref.dtype)
        return carry

    lax.fori_loop(0, n_blk, q_block, 0)


def _fox_prompt(q, kb, vb, c, ct, n_heads):
    n_seq, seq, width = q.shape
    blk = ATTN_BLOCK
    qkv = pl.BlockSpec((None, seq, LANES), lambda b, p: (b, 0, p))
    return pl.pallas_call(
        functools.partial(_fox_body, blk=blk, n_blk=seq // blk),
        grid=(n_seq, width // LANES),
        in_specs=[qkv, qkv, qkv,
                  pl.BlockSpec((None, seq, n_heads), lambda b, p: (b, 0, 0)),
                  pl.BlockSpec((None, n_heads, seq), lambda b, p: (b, 0, 0))],
        out_specs=qkv,
        out_shape=jax.ShapeDtypeStruct((n_seq, seq, width), BF16),
        scratch_shapes=[pltpu.VMEM((2, blk, 1), F32), pltpu.VMEM((2, blk, 1), F32), pltpu.VMEM((2, blk, LANES), F32)],
        compiler_params=_params(2),
        name="fox_prompt",
    )(q, kb, vb, c, ct)


def _memkv_body(mem_ref, g_ref, w_ref, gk_ref, bd_ref, mk_ref, mv_ref):
    xn = _rms(mem_ref[...], g_ref[...]).astype(BF16)
    kv = _dot(xn, w_ref[...])
    half = kv.shape[1] // 2
    mk_ref[...] = _head_norm(kv[:, :half], gk_ref[...], bd_ref)
    mv_ref[...] = kv[:, half:]


def _mem_kv(mem, g_mem, w_kv, g_k, bd):
    n_seq, m_tok, d = mem.shape
    depth, _, two_w = w_kv.shape
    w = two_w // 2
    out = (pl.BlockSpec((None, None, m_tok, w), lambda l, b: (b, l, 0, 0)),
           jax.ShapeDtypeStruct((n_seq, depth, m_tok, w), F32))
    return pl.pallas_call(
        _memkv_body,
        grid=(depth, n_seq),
        in_specs=[pl.BlockSpec((None, m_tok, d), lambda l, b: (b, 0, 0)),
                  pl.BlockSpec((None, 1, d), lambda l, b: (l, 0, 0)),
                  pl.BlockSpec((None, d, two_w), lambda l, b: (l, 0, 0)),
                  pl.BlockSpec((None, 1, w), lambda l, b: (l, 0, 0)),
                  pl.BlockSpec((256, 256), lambda l, b: (0, 0))],
        out_specs=[out[0], out[0]],
        out_shape=[out[1], out[1]],
        compiler_params=_params(2),
        name="mem_kv",
    )(mem, g_mem, w_kv, g_k, bd)


def _pool_window_lanes(shape, axis):
    lane = lax.broadcasted_iota(jnp.int32, shape, axis)
    w = jnp.full(shape, POOL_WINDOWS[-1], jnp.int32)
    for g in range(len(POOL_WINDOWS) - 2, -1, -1):
        w = jnp.where(lane < (g + 1) * POOL_GROUP, POOL_WINDOWS[g], w)
    return w


def _pair_attend(q2, k2, v2):
    lane = lax.broadcasted_iota(jnp.int32, q2.shape, 1)
    zero = jnp.zeros_like(q2)
    outs = []
    for e in range(2):
        qe = jnp.where((lane >= HEAD_DIM) == bool(e), q2, zero)
        s = _dot_nt(qe, k2)
        p = jnp.exp(s - jnp.max(s, axis=1, keepdims=True))
        outs.append(_dot(p.astype(BF16), v2) / jnp.sum(p, axis=1, keepdims=True))
    return jnp.where(lane < HEAD_DIM, outs[0], outs[1])


def _mix_body(x_ref, up_ref, upp_ref, of_ref, mq_ref, mk_ref, mv_ref, wp_ref, ps_ref, wo_ref, o_ref, z_ref,
              *, tm, tiles_per_seq):
    tile = pl.program_id(0) % tiles_per_seq
    z = up_ref[...]
    z_ref[0:HALO, :] = jnp.where(tile == 0, 0.0, upp_ref[...])
    z_ref[HALO:, :] = z
    win = _pool_window_lanes(z.shape, 1)
    run = z
    wsum = jnp.zeros_like(z)
    for s in range(1, POOL_WINDOWS[-1]):
        run = run + z_ref[pl.ds(HALO - s, tm), :]
        if s + 1 in POOL_WINDOWS:
            wsum = jnp.where(win == s + 1, run, wsum)
    pos = tile * tm + lax.broadcasted_iota(jnp.int32, z.shape, 0)
    cnt = jnp.minimum(pos + 1, win).astype(F32)
    pooled = wsum / cnt - z
    o_pool = _dot(pooled.astype(BF16), wp_ref[...]) * ps_ref[...]

    o_mem = jnp.concatenate(
        [_pair_attend(mq_ref[:, o:o + LANES], mk_ref[:, o:o + LANES].astype(BF16), mv_ref[:, o:o + LANES].astype(BF16))
         for o in range(0, mq_ref.shape[1], LANES)], axis=1)

    n_pool, n_fox = o_pool.shape[1], of_ref.shape[1]
    y = _dot(o_pool.astype(BF16), wo_ref[0:n_pool, :])
    y = y + _dot(of_ref[...], wo_ref[n_pool:n_pool + n_fox, :])
    y = y + _dot(o_mem.astype(BF16), wo_ref[n_pool + n_fox:, :])
    o_ref[...] = x_ref[...] + y


def _mix_prompt(x, u_pool, o_fox, mq, mem_k, mem_v, wts, layer, seq):
    t, d = x.shape
    tm = TOKEN_TILE
    tps = seq // tm
    halo_blocks = tm // HALO
    mem_spec = pl.BlockSpec((None, None) + mem_k.shape[2:], lambda i: (i // tps, layer, 0, 0))
    return pl.pallas_call(
        functools.partial(_mix_body, tm=tm, tiles_per_seq=tps),
        grid=(t // tm,),
        in_specs=[
            pl.BlockSpec((tm, d), lambda i: (i, 0)),
            pl.BlockSpec((tm, POOL_WIDTH), lambda i: (i, 0)),
            pl.BlockSpec((HALO, POOL_WIDTH), lambda i: (jnp.maximum(i * halo_blocks - 1, 0), 0)),
            pl.BlockSpec((tm, o_fox.shape[1]), lambda i: (i, 0)),
            pl.BlockSpec((tm, mq.shape[1]), lambda i: (i, 0)),
            mem_spec, mem_spec,
            _resident((None, POOL_WIDTH, POOL_WIDTH), lambda i: (layer, 0, 0)),
            pl.BlockSpec((None, 1, POOL_WIDTH), lambda i: (layer, 0, 0)),
            _resident((None, d, d), lambda i: (layer, 0, 0)),
        ],
        out_specs=pl.BlockSpec((tm, d), lambda i: (i, 0)),
        out_shape=jax.ShapeDtypeStruct((t, d), F32),
        scratch_shapes=[pltpu.VMEM((tm + HALO, POOL_WIDTH), F32)],
        compiler_params=_params(1),
        name="mix_prompt",
    )(x, u_pool, u_pool, o_fox, mq, mem_k, mem_v, wts["w_pool"], wts["pool_scale"], wts["w_out"])


def _mix_sample_body(x_ref, u_ref, st_ref, of_ref, om_ref, wp_ref, ps_ref, wo_ref, o_ref):
    u = u_ref[...]
    st = st_ref[...]
    row = lax.broadcasted_iota(jnp.int32, st.shape, 1)
    win3 = _pool_window_lanes(st.shape, 2)
    hist = jnp.sum(jnp.where(row >= POOL_HIST + 1 - win3, st, 0.0), axis=1)
    win = _pool_window_lanes(u.shape, 1).astype(F32)
    pooled = (hist + u) / win - u
    o_pool = _dot(pooled.astype(BF16), wp_ref[...]) * ps_ref[...]
    n_pool, n_fox = o_pool.shape[1], of_ref.shape[1]
    y = _dot(o_pool.astype(BF16), wo_ref[0:n_pool, :])
    y = y + _dot(of_ref[...].astype(BF16), wo_ref[n_pool:n_pool + n_fox, :])
    y = y + _dot(om_ref[...].astype(BF16), wo_ref[n_pool + n_fox:, :])
    o_ref[...] = x_ref[...] + y


def _mix_sample(x, u_pool, state_pool, o_fox, o_mem, wts, layer):
    t, d = x.shape
    full = lambda a: pl.BlockSpec(a.shape, lambda i: (0,) * a.ndim)
    return pl.pallas_call(
        _mix_sample_body,
        grid=(1,),
        in_specs=[
            full(x), full(u_pool),
            pl.BlockSpec((t, None) + state_pool.shape[2:], lambda i: (0, layer, 0, 0)),
            full(o_fox), full(o_mem),
            pl.BlockSpec((None, POOL_WIDTH, POOL_WIDTH), lambda i: (layer, 0, 0)),
            pl.BlockSpec((None, 1, POOL_WIDTH), lambda i: (layer, 0, 0)),
            pl.BlockSpec((None, d, d), lambda i: (layer, 0, 0)),
        ],
        out_specs=full(x),
        out_shape=jax.ShapeDtypeStruct((t, d), F32),
        compiler_params=_params(1),
        name="mix_sample",
    )(x, u_pool, state_pool, o_fox, o_mem, wts["w_pool"], wts["pool_scale"], wts["w_out"])


def _rows_to_heads(per_head, n_rows, width):
    row = lax.broadcasted_iota(jnp.int32, (n_rows, width), 0)
    out = per_head[0]
    for h in range(1, len(per_head)):
        out = jnp.where(row == h, per_head[h], out)
    return out


def _decode_body(pt_ref, q_ref, ks_ref, vs_ref, lfs_ref, mq_ref, ck_hbm, cv_hbm, cf_hbm, mk_ref, mv_ref,
                 of_ref, om_ref, kbuf, vbuf, fbuf, sem, *, layer, n_pages, page, n_heads, n_mem_heads):
    b = pl.program_id(0)
    n_samples = pl.num_programs(0)
    slot = b % 2

    def page_copies(sample, sl):
        cps = []
        for j in range(n_pages):
            pg = pt_ref[sample, j]
            cps.append(pltpu.make_async_copy(ck_hbm.at[pg, layer], kbuf.at[sl, j], sem.at[sl, 0]))
            cps.append(pltpu.make_async_copy(cv_hbm.at[pg, layer], vbuf.at[sl, j], sem.at[sl, 1]))
            cps.append(pltpu.make_async_copy(cf_hbm.at[pg, layer], fbuf.at[sl, j], sem.at[sl, 2]))
        return cps

    @pl.when(b == 0)
    def _():
        for cp in page_copies(0, 0):
            cp.start()

    @pl.when(b + 1 < n_samples)
    def _():
        for cp in page_copies(b + 1, 1 - slot):
            cp.start()

    for cp in page_copies(b, slot):
        cp.wait()

    past = n_pages * page
    q8 = q_ref[...]
    qb = q8.astype(BF16)
    scores = _rows_to_heads(
        [_dot_nt(qb, kbuf[slot, :, :, h, :].reshape(past, HEAD_DIM).astype(BF16)) for h in range(n_heads)],
        n_heads, past)

    ft = fbuf[slot].reshape(past, n_heads).T
    lane = lax.broadcasted_iota(jnp.int32, ft.shape, 1)
    suffix = ft
    shift = 1
    while shift < past:
        suffix = suffix + jnp.where(lane + shift < past, pltpu.roll(suffix, past - shift, axis=1), 0.0)
        shift *= 2
    s_past = scores + (suffix - ft) + lfs_ref[...]
    s_self = jnp.sum(q8 * ks_ref[...], axis=1, keepdims=True)
    m = jnp.maximum(jnp.max(s_past, axis=1, keepdims=True), s_self)
    p = jnp.exp(s_past - m)
    p_self = jnp.exp(s_self - m)
    denom = jnp.sum(p, axis=1, keepdims=True) + p_self
    pb = p.astype(BF16)
    pv = _rows_to_heads(
        [_dot(pb, vbuf[slot, :, :, h, :].reshape(past, HEAD_DIM).astype(BF16)) for h in range(n_heads)],
        n_heads, HEAD_DIM)
    of_ref[...] = (pv + p_self * vs_ref[...]) / denom

    mqb = mq_ref[...].astype(BF16)
    sm = _rows_to_heads([_dot_nt(mqb, mk_ref[:, h, :].astype(BF16)) for h in range(n_mem_heads)],
                        mqb.shape[0], mk_ref.shape[0])
    pm = jnp.exp(sm - jnp.max(sm, axis=1, keepdims=True))
    pmb = pm.astype(BF16)
    om = _rows_to_heads([_dot(pmb, mv_ref[:, h, :].astype(BF16)) for h in range(n_mem_heads)],
                        mqb.shape[0], HEAD_DIM)
    om_ref[...] = om / jnp.sum(pm, axis=1, keepdims=True)


def _decode(page_table, q, k_new, v_new, lf_new, mq, cache_k, cache_v, cache_logf, cache_mem_k, cache_mem_v, layer):
    n_samples, n_pages = page_table.shape
    _, depth, page, n_heads, hd = cache_k.shape
    _, _, m_tok, n_mem_heads, _ = cache_mem_k.shape
    per_sample = lambda rows, w: pl.BlockSpec((None, rows, w), lambda b, pt: (b, 0, 0))
    mem_spec = pl.BlockSpec((None, None, m_tok, n_mem_heads, hd), lambda b, pt: (b, layer, 0, 0, 0))
    any_spec = pl.BlockSpec(memory_space=pl.ANY)
    grid_spec = pltpu.PrefetchScalarGridSpec(
        num_scalar_prefetch=1,
        grid=(n_samples,),
        in_specs=[per_sample(n_heads, hd), per_sample(n_heads, hd), per_sample(n_heads, hd), per_sample(n_heads, 1),
                  per_sample(n_heads, hd), any_spec, any_spec, any_spec, mem_spec, mem_spec],
        out_specs=[per_sample(n_heads, hd), per_sample(n_heads, hd)],
        scratch_shapes=[pltpu.VMEM((2, n_pages, page, n_heads, hd), F32),
                        pltpu.VMEM((2, n_pages, page, n_heads, hd), F32),
                        pltpu.VMEM((2, n_pages, page, n_heads), F32),
                        pltpu.SemaphoreType.DMA((2, 3))],
    )
    return pl.pallas_call(
        functools.partial(_decode_body, layer=layer, n_pages=n_pages, page=page, n_heads=n_heads,
                          n_mem_heads=n_mem_heads),
        grid_spec=grid_spec,
        out_shape=[jax.ShapeDtypeStruct((n_samples, n_heads, hd), F32)] * 2,
        compiler_params=_params(1),
        name="decode",
    )(page_table, q, k_new, v_new, lf_new, mq, cache_k, cache_v, cache_logf, cache_mem_k, cache_mem_v)


def _block_diag(blocks):
    n, r, c = blocks.shape
    eye = jnp.eye(n, dtype=blocks.dtype)
    return (eye[:, None, :, None] * blocks[:, :, None, :]).reshape(n * r, n * c)


def kernel(x_prompt, x_sample, cache_k, cache_v, cache_logf, state_pool, cache_mem_k, cache_mem_v, page_table, mem_prompt, g_ffn1, w_ffn1_gu, w_ffn1_down, g_mix, w_in, b_forget, g_fox_q, g_fox_k, w_pool, pool_scale, g_mem, w_mem_kv, g_mem_q, g_mem_k, w_out, g_ffn2, w_ffn2_gu, w_ffn2_down):
    n_seq, seq, d = x_prompt.shape
    n_samples = x_sample.shape[0]
    depth = w_in.shape[0]
    n_heads = cache_k.shape[3]
    n_mem_heads = cache_mem_k.shape[3]
    fox_w = n_heads * HEAD_DIM
    mem_w = n_mem_heads * HEAD_DIM
    assert x_sample.shape[1] == 1 and seq % TOKEN_TILE == 0 and seq % ATTN_BLOCK == 0

    o_f = POOL_WIDTH + 3 * fox_w
    w_in_r = jnp.concatenate(
        [w_in[:, :, :o_f], w_in[:, :, o_f + n_heads:], w_in[:, :, o_f:o_f + n_heads],
         jnp.zeros((depth, d, LANES - n_heads), w_in.dtype)], axis=2).astype(BF16)
    row3 = lambda a: a.reshape(depth, 1, -1)
    wts = {
        "g_mix": row3(g_mix),
        "w_in": w_in_r,
        "b_f": jnp.pad(b_forget, ((0, 0), (0, LANES - n_heads))).reshape(depth, 1, LANES),
        "g_q": row3(jnp.tile(g_fox_q, (1, n_heads))),
        "g_k": row3(jnp.tile(g_fox_k, (1, n_heads))),
        "g_mq": row3(jnp.tile(g_mem_q, (1, n_mem_heads))),
        "bd": _block_diag(jnp.full((256 // HEAD_DIM, HEAD_DIM, HEAD_DIM), 1.0 / HEAD_DIM, BF16)),
        "w_pool": jax.vmap(_block_diag)(w_pool).astype(BF16),
        "pool_scale": row3(pool_scale),
        "w_out": w_out.astype(BF16),
    }
    ffn1 = (row3(g_ffn1), w_ffn1_gu.astype(BF16), w_ffn1_down.astype(BF16))
    ffn2 = (row3(g_ffn2), w_ffn2_gu.astype(BF16), w_ffn2_down.astype(BF16))

    mem_k, mem_v = _mem_kv(mem_prompt, row3(g_mem), w_mem_kv.astype(BF16),
                           row3(jnp.tile(g_mem_k, (1, n_mem_heads))), wts["bd"])

    xp = x_prompt.reshape(n_seq * seq, d)
    xs = x_sample.reshape(n_samples, d)
    kp_l, vp_l, fp_l, pp_l, ks_l, vs_l, fs_l, ps_l = [], [], [], [], [], [], [], []
    for layer in range(depth):
        xp = _ffn(xp, *ffn1, layer, TOKEN_TILE)
        u_pool, q, k, kb, v, vb, logf, mq, c, ct = _proj_prompt(xp, wts, layer, n_seq, seq, n_heads)
        seq3 = lambda a: a.reshape(n_seq, seq, a.shape[1])
        o_fox = _fox_prompt(seq3(q), seq3(kb), seq3(vb), seq3(c), ct, n_heads)
        xp = _mix_prompt(xp, u_pool, o_fox.reshape(n_seq * seq, fox_w), mq, mem_k, mem_v, wts, layer, seq)
        xp = _ffn(xp, *ffn2, layer, TOKEN_TILE)
        kp_l.append(k.reshape(n_seq, seq, n_heads, HEAD_DIM))
        vp_l.append(v.reshape(n_seq, seq, n_heads, HEAD_DIM))
        fp_l.append(logf.reshape(n_seq, seq, n_heads))
        pp_l.append(seq3(u_pool)[:, seq - POOL_HIST:])

        xs = _ffn(xs, *ffn1, layer, n_samples)
        u_s, q_s, k_s, v_s, lf_s, mq_s = _proj_sample(xs, wts, layer, n_heads)
        heads = lambda a, h: a.reshape(n_samples, h, HEAD_DIM)
        mq_pad = jnp.pad(heads(mq_s, n_mem_heads), ((0, 0), (0, SUBLANES - n_mem_heads), (0, 0)))
        o_fox_s, o_mem_s = _decode(page_table, heads(q_s, n_heads), heads(k_s, n_heads), heads(v_s, n_heads),
                                   lf_s.reshape(n_samples, n_heads, 1), mq_pad,
                                   cache_k, cache_v, cache_logf, cache_mem_k, cache_mem_v, layer)
        xs = _mix_sample(xs, u_s, state_pool, o_fox_s.reshape(n_samples, fox_w),
                         o_mem_s[:, :n_mem_heads].reshape(n_samples, mem_w), wts, layer)
        xs = _ffn(xs, *ffn2, layer, n_samples)
        ks_l.append(k_s.reshape(n_samples, 1, n_heads, HEAD_DIM))
        vs_l.append(v_s.reshape(n_samples, 1, n_heads, HEAD_DIM))
        fs_l.append(lf_s.reshape(n_samples, 1, n_heads))
        ps_l.append(jnp.concatenate([state_pool[:, layer, 1:], u_s[:, None, :]], axis=1))

    stack = lambda xs_: jnp.stack(xs_, axis=1)
    mem_shape = (n_seq, depth, mem_k.shape[2], n_mem_heads, HEAD_DIM)
    return (xp.reshape(n_seq, seq, d), xs.reshape(n_samples, 1, d),
            stack(kp_l), stack(vp_l), stack(fp_l), stack(pp_l),
            mem_k.reshape(mem_shape), mem_v.reshape(mem_shape),
            stack(ks_l), stack(vs_l), stack(fs_l), stack(ps_l))
```

```python
import functools

import jax
import jax.numpy as jnp
from jax import lax
from jax.experimental import pallas as pl
from jax.experimental.pallas import tpu as pltpu

F32 = jnp.float32
BF16 = jnp.bfloat16

HEAD_DIM = 64
POOL_WINDOWS = (2, 4, 8, 16)
POOL_GROUP = 64
POOL_WIDTH = len(POOL_WINDOWS) * POOL_GROUP
POOL_HIST = max(POOL_WINDOWS) - 1
EPS = 1e-6
SCALE = HEAD_DIM ** -0.5

LANES = 128
SUBLANES = 8
VMEM_LIMIT_BYTES = 56 * 1024 * 1024
NEG_BIG = -1e30

TOKEN_TILE = 512
FFN_CHUNK = 256
ATTN_BLOCK = 256
HALO = 16
NORM_TILE = 256


def _dot(a, b):
    return jnp.dot(a, b, preferred_element_type=F32)


def _dot_nt(a, b):
    return lax.dot_general(a, b, (((1,), (1,)), ((), ())), preferred_element_type=F32)


def _rms(x, g):
    return x * lax.rsqrt(jnp.mean(x * x, axis=-1, keepdims=True) + EPS) * g


def _params(n_axes):
    return pltpu.CompilerParams(dimension_semantics=("arbitrary",) * n_axes, vmem_limit_bytes=VMEM_LIMIT_BYTES)


def _resident(shape, index_map):
    return pl.BlockSpec(shape, index_map, pipeline_mode=pl.Buffered(1))


def _head_norm(u, g, bd_ref):
    sq = (u * u).astype(BF16)
    ms = jnp.concatenate([_dot(sq[:, o:o + NORM_TILE], bd_ref[...]) for o in range(0, u.shape[1], NORM_TILE)], axis=1)
    return u * lax.rsqrt(ms + EPS) * g


def _head_norm_t(ut, g_col):
    rows, cols = ut.shape
    u3 = ut.reshape(rows // HEAD_DIM, HEAD_DIM, cols)
    ms = jnp.mean(u3 * u3, axis=1, keepdims=True)
    return (u3 * lax.rsqrt(ms + EPS) * g_col[None]).reshape(rows, cols)


def _ffn_body(x_ref, g_ref, wgu_ref, wd_ref, o_ref, *, d_ff):
    x = x_ref[...]
    xn = _rms(x, g_ref[...]).astype(BF16)
    acc = None
    for c in range(d_ff // FFN_CHUNK):
        lo = c * FFN_CHUNK
        gate = _dot(xn, wgu_ref[:, lo:lo + FFN_CHUNK])
        up = _dot(xn, wgu_ref[:, d_ff + lo:d_ff + lo + FFN_CHUNK])
        act = (gate * jax.nn.sigmoid(gate) * up).astype(BF16)
        part = _dot(act, wd_ref[lo:lo + FFN_CHUNK, :])
        acc = part if acc is None else acc + part
    o_ref[...] = x + 0.5 * acc


def _ffn(x, g, wgu, wd, layer, tm):
    t, d = x.shape
    d_ff = wd.shape[1]
    return pl.pallas_call(
        functools.partial(_ffn_body, d_ff=d_ff),
        grid=(t // tm,),
        in_specs=[
            pl.BlockSpec((tm, d), lambda i: (i, 0)),
            pl.BlockSpec((None, 1, d), lambda i: (layer, 0, 0)),
            _resident((None, d, 2 * d_ff), lambda i: (layer, 0, 0)),
            _resident((None, d_ff, d), lambda i: (layer, 0, 0)),
        ],
        out_specs=pl.BlockSpec((tm, d), lambda i: (i, 0)),
        out_shape=jax.ShapeDtypeStruct((t, d), F32),
        compiler_params=_params(1),
        name="ffn",
    )(x, g, wgu, wd)


def _proj_rows(x_ref, g_ref, w_ref, bf_ref, gq_ref, gmq_ref, bd_ref):
    xn = _rms(x_ref[...], g_ref[...]).astype(BF16)
    u = _dot(xn, w_ref[...])
    n_q = gq_ref.shape[1]
    o_q, o_mq = POOL_WIDTH, POOL_WIDTH + n_q
    o_f = o_mq + gmq_ref.shape[1]
    u_pool = u[:, :o_q]
    q = _head_norm(u[:, o_q:o_mq], gq_ref[...], bd_ref) * SCALE
    mq = _head_norm(u[:, o_mq:o_f], gmq_ref[...], bd_ref) * SCALE
    logf = jax.nn.log_sigmoid(u[:, o_f:] + bf_ref[...])
    return xn, u_pool, q, mq, logf


def _proj_prompt_body(x_ref, g_ref, w_ref, wkv_ref, bf_ref, gq_ref, gmq_ref, gk_ref, bd_ref, kt_in, vt_in,
                      up_ref, q_ref, mq_ref, kt_ref, vt_ref, ktb_ref, vtb_ref, lft_ref, ct_ref, carry_ref,
                      *, tm, tiles_per_seq, n_heads):
    del kt_in, vt_in
    xn, u_pool, q, mq, logf = _proj_rows(x_ref, g_ref, w_ref, bf_ref, gq_ref, gmq_ref, bd_ref)
    up_ref[...] = u_pool
    q_ref[...] = q.astype(BF16)
    mq_ref[...] = mq.astype(BF16)

    kvt = _dot_nt(wkv_ref[...], xn)
    half = kvt.shape[0] // 2
    kt = _head_norm_t(kvt[:half], gk_ref[...])
    vt = kvt[half:]
    kt_ref[...] = kt
    vt_ref[...] = vt
    ktb_ref[...] = kt.astype(BF16)
    vtb_ref[...] = vt.astype(BF16)

    @pl.when(pl.program_id(0) % tiles_per_seq == 0)
    def _():
        carry_ref[...] = jnp.zeros_like(carry_ref)

    lane_f = lax.broadcasted_iota(jnp.int32, logf.shape, 1)
    ft = jnp.where(lane_f < n_heads, logf, 0.0).T
    lft_ref[...] = ft[:n_heads]
    lane = lax.broadcasted_iota(jnp.int32, ft.shape, 1)
    shift = 1
    while shift < tm:
        ft = ft + jnp.where(lane >= shift, pltpu.roll(ft, shift, axis=1), 0.0)
        shift *= 2
    ct = ft + jnp.concatenate([carry_ref[...]] * (tm // LANES), axis=1)
    carry_ref[...] = jnp.broadcast_to(ct[:, tm - 1:tm], carry_ref.shape)
    ct_ref[...] = ct[:n_heads]


def _proj_prompt(x, wts, layer, n_seq, seq, n_heads, kv_bufs):
    t, d = x.shape
    tm = TOKEN_TILE
    tps = seq // tm
    depth, kv_rows, _ = wts["w_kv_t"].shape
    width = kv_rows // 2
    row = lambda w, dt: (pl.BlockSpec((tm, w), lambda i: (i, 0)), jax.ShapeDtypeStruct((t, w), dt))
    feat = lambda dt: (pl.BlockSpec((None, width, tm), lambda i: (i // tps, 0, i % tps)),
                       jax.ShapeDtypeStruct((n_seq, width, seq), dt))
    stacked = (pl.BlockSpec((None, None, width, tm), lambda i: (i // tps, layer, 0, i % tps)),
               jax.ShapeDtypeStruct((n_seq, depth, width, seq), F32))
    head_rows = (pl.BlockSpec((None, n_heads, tm), lambda i: (i // tps, 0, i % tps)),
                 jax.ShapeDtypeStruct((n_seq, n_heads, seq), F32))
    outs = [row(POOL_WIDTH, F32), row(width, BF16), row(wts["g_mq"].shape[2], BF16),
            stacked, stacked, feat(BF16), feat(BF16), head_rows, head_rows]
    in_specs = [
        pl.BlockSpec((tm, d), lambda i: (i, 0)),
        pl.BlockSpec((None, 1, d), lambda i: (layer, 0, 0)),
        _resident((None, d, wts["w_rows"].shape[2]), lambda i: (layer, 0, 0)),
        _resident((None, kv_rows, d), lambda i: (layer, 0, 0)),
        pl.BlockSpec((None, 1, LANES), lambda i: (layer, 0, 0)),
        pl.BlockSpec((None, 1, width), lambda i: (layer, 0, 0)),
        pl.BlockSpec((None, 1, wts["g_mq"].shape[2]), lambda i: (layer, 0, 0)),
        pl.BlockSpec((None, HEAD_DIM, 1), lambda i: (layer, 0, 0)),
        _resident((NORM_TILE, NORM_TILE), lambda i: (0, 0)),
        pl.BlockSpec(memory_space=pl.ANY),
        pl.BlockSpec(memory_space=pl.ANY),
    ]
    return pl.pallas_call(
        functools.partial(_proj_prompt_body, tm=tm, tiles_per_seq=tps, n_heads=n_heads),
        grid=(t // tm,),
        in_specs=in_specs,
        out_specs=[o[0] for o in outs],
        out_shape=[o[1] for o in outs],
        scratch_shapes=[pltpu.VMEM((LANES, LANES), F32)],
        input_output_aliases={9: 3, 10: 4},
        compiler_params=_params(1),
        name="proj_prompt",
    )(x, wts["g_mix"], wts["w_rows"], wts["w_kv_t"], wts["b_f"], wts["g_q"], wts["g_mq"], wts["g_k_col"], wts["bd"],
      *kv_bufs)


def _proj_sample_body(x_ref, g_ref, w_ref, wkv_ref, bf_ref, gq_ref, gmq_ref, gk_ref, bd_ref,
                      up_ref, q_ref, mq_ref, k_ref, v_ref, lf_ref, *, n_heads):
    xn, u_pool, q, mq, logf = _proj_rows(x_ref, g_ref, w_ref, bf_ref, gq_ref, gmq_ref, bd_ref)
    up_ref[...] = u_pool
    q_ref[...] = q
    mq_ref[...] = mq
    lf_ref[...] = logf[:, :n_heads]
    kv = _dot(xn, wkv_ref[...])
    half = kv.shape[1] // 2
    k_ref[...] = _head_norm(kv[:, :half], gk_ref[...], bd_ref)
    v_ref[...] = kv[:, half:]


def _proj_sample(x, wts, layer, n_heads):
    t, d = x.shape
    width = wts["g_q"].shape[2]
    mem_w = wts["g_mq"].shape[2]
    row = lambda w: (pl.BlockSpec((t, w), lambda i: (0, 0)), jax.ShapeDtypeStruct((t, w), F32))
    outs = [row(POOL_WIDTH), row(width), row(mem_w), row(width), row(width), row(n_heads)]
    lay = lambda shape: pl.BlockSpec((None,) + shape, lambda i: (layer, 0, 0))
    return pl.pallas_call(
        functools.partial(_proj_sample_body, n_heads=n_heads),
        grid=(1,),
        in_specs=[pl.BlockSpec((t, d), lambda i: (0, 0)), lay((1, d)), lay((d, wts["w_rows"].shape[2])),
                  lay((d, 2 * width)), lay((1, LANES)), lay((1, width)), lay((1, mem_w)), lay((1, width)),
                  pl.BlockSpec((NORM_TILE, NORM_TILE), lambda i: (0, 0))],
        out_specs=[o[0] for o in outs],
        out_shape=[o[1] for o in outs],
        compiler_params=_params(1),
        name="proj_sample",
    )(x, wts["g_mix"], wts["w_rows"], wts["w_kv"], wts["b_f"], wts["g_q"], wts["g_mq"], wts["g_k"], wts["bd"])


def _fox_body(q_ref, kt_ref, vt_ref, ct_ref, o_ref, *, blk, n_blk):
    pair = pl.program_id(1)
    lane = lax.broadcasted_iota(jnp.int32, (blk, LANES), 1)
    causal = lax.broadcasted_iota(jnp.int32, (blk, blk), 0) >= lax.broadcasted_iota(jnp.int32, (blk, blk), 1)
    ck_rows = [ct_ref[pl.ds(2 * pair + e, 1), :] for e in range(2)]

    for i in range(n_blk):
        n0, n1 = i * blk, (i + 1) * blk
        q2 = q_ref[n0:n1, :]
        zero = jnp.zeros_like(q2)
        outs = []
        for e in range(2):
            qe = jnp.where((lane >= HEAD_DIM) == bool(e), q2, zero)
            ck = ck_rows[e][:, :n1] - ck_rows[e][:, n0:n0 + 1]
            s_d = jnp.where(causal, _dot(qe, kt_ref[:, n0:n1]) - ck[:, n0:n1], NEG_BIG)
            m = jnp.max(s_d, axis=1, keepdims=True)
            if i:
                s_f = _dot(qe, kt_ref[:, :n0]) - ck[:, :n0]
                m = jnp.maximum(m, jnp.max(s_f, axis=1, keepdims=True))
            p_d = jnp.exp(s_d - m)
            den = jnp.sum(p_d, axis=1, keepdims=True)
            o = _dot_nt(p_d.astype(BF16), vt_ref[:, n0:n1])
            if i:
                p_f = jnp.exp(s_f - m)
                den = den + jnp.sum(p_f, axis=1, keepdims=True)
                o = o + _dot_nt(p_f.astype(BF16), vt_ref[:, :n0])
            outs.append(o / den)
        o_ref[n0:n1, :] = jnp.where(lane < HEAD_DIM, outs[0], outs[1]).astype(o_ref.dtype)


def _fox_prompt(q, ktb, vtb, ct, n_heads):
    n_seq, seq, width = q.shape
    blk = ATTN_BLOCK
    rows = pl.BlockSpec((None, seq, LANES), lambda b, p: (b, 0, p))
    feats = pl.BlockSpec((None, LANES, seq), lambda b, p: (b, p, 0))
    return pl.pallas_call(
        functools.partial(_fox_body, blk=blk, n_blk=seq // blk),
        grid=(n_seq, width // LANES),
        in_specs=[rows, feats, feats, pl.BlockSpec((None, n_heads, seq), lambda b, p: (b, 0, 0))],
        out_specs=rows,
        out_shape=jax.ShapeDtypeStruct((n_seq, seq, width), BF16),
        compiler_params=_params(2),
        name="fox_prompt",
    )(q, ktb, vtb, ct)


def _memkv_body(mem_ref, g_ref, w_ref, gk_ref, mk_ref, mv_ref):
    xn = _rms(mem_ref[...], g_ref[...]).astype(BF16)
    kvt = _dot_nt(w_ref[...], xn)
    half = kvt.shape[0] // 2
    mk_ref[...] = _head_norm_t(kvt[:half], gk_ref[...])
    mv_ref[...] = kvt[half:]


def _mem_kv(mem, g_mem, w_kv_t, g_k_col):
    n_seq, m_tok, d = mem.shape
    depth, two_w, _ = w_kv_t.shape
    w = two_w // 2
    out = (pl.BlockSpec((None, None, w, m_tok), lambda l, b: (b, l, 0, 0)),
           jax.ShapeDtypeStruct((n_seq, depth, w, m_tok), F32))
    return pl.pallas_call(
        _memkv_body,
        grid=(depth, n_seq),
        in_specs=[pl.BlockSpec((None, m_tok, d), lambda l, b: (b, 0, 0)),
                  pl.BlockSpec((None, 1, d), lambda l, b: (l, 0, 0)),
                  pl.BlockSpec((None, two_w, d), lambda l, b: (l, 0, 0)),
                  pl.BlockSpec((None, HEAD_DIM, 1), lambda l, b: (l, 0, 0))],
        out_specs=[out[0], out[0]],
        out_shape=[out[1], out[1]],
        compiler_params=_params(2),
        name="mem_kv",
    )(mem, g_mem, w_kv_t, g_k_col)


def _pool_window_lanes(shape, axis):
    lane = lax.broadcasted_iota(jnp.int32, shape, axis)
    w = jnp.full(shape, POOL_WINDOWS[-1], jnp.int32)
    for g in range(len(POOL_WINDOWS) - 2, -1, -1):
        w = jnp.where(lane < (g + 1) * POOL_GROUP, POOL_WINDOWS[g], w)
    return w


def _pair_attend(q2, kt2, vt2):
    lane = lax.broadcasted_iota(jnp.int32, q2.shape, 1)
    zero = jnp.zeros_like(q2)
    outs = []
    for e in range(2):
        qe = jnp.where((lane >= HEAD_DIM) == bool(e), q2, zero)
        s = _dot(qe, kt2)
        p = jnp.exp(s - jnp.max(s, axis=1, keepdims=True))
        outs.append(_dot_nt(p.astype(BF16), vt2) / jnp.sum(p, axis=1, keepdims=True))
    return jnp.where(lane < HEAD_DIM, outs[0], outs[1])


def _mix_body(x_ref, up_ref, upp_ref, of_ref, mq_ref, mk_ref, mv_ref, wp_ref, ps_ref, wo_ref, o_ref, z_ref,
              *, tm, tiles_per_seq):
    tile = pl.program_id(0) % tiles_per_seq
    z = up_ref[...]
    z_ref[0:HALO, :] = jnp.where(tile == 0, 0.0, upp_ref[...])
    z_ref[HALO:, :] = z
    win = _pool_window_lanes(z.shape, 1)
    run = z
    wsum = jnp.zeros_like(z)
    for s in range(1, POOL_WINDOWS[-1]):
        run = run + z_ref[pl.ds(HALO - s, tm), :]
        if s + 1 in POOL_WINDOWS:
            wsum = jnp.where(win == s + 1, run, wsum)
    pos = tile * tm + lax.broadcasted_iota(jnp.int32, z.shape, 0)
    cnt = jnp.minimum(pos + 1, win).astype(F32)
    pooled = wsum / cnt - z
    o_pool = _dot(pooled.astype(BF16), wp_ref[...]) * ps_ref[...]

    o_mem = jnp.concatenate(
        [_pair_attend(mq_ref[:, o:o + LANES], mk_ref[o:o + LANES, :].astype(BF16), mv_ref[o:o + LANES, :].astype(BF16))
         for o in range(0, mq_ref.shape[1], LANES)], axis=1)

    n_pool, n_fox = o_pool.shape[1], of_ref.shape[1]
    y = _dot(o_pool.astype(BF16), wo_ref[0:n_pool, :])
    y = y + _dot(of_ref[...], wo_ref[n_pool:n_pool + n_fox, :])
    y = y + _dot(o_mem.astype(BF16), wo_ref[n_pool + n_fox:, :])
    o_ref[...] = x_ref[...] + y


def _mix_prompt(x, u_pool, o_fox, mq, mem_kt, mem_vt, wts, layer, seq):
    t, d = x.shape
    tm = TOKEN_TILE
    tps = seq // tm
    halo_blocks = tm // HALO
    mem_spec = pl.BlockSpec((None, None) + mem_kt.shape[2:], lambda i: (i // tps, layer, 0, 0))
    return pl.pallas_call(
        functools.partial(_mix_body, tm=tm, tiles_per_seq=tps),
        grid=(t // tm,),
        in_specs=[
            pl.BlockSpec((tm, d), lambda i: (i, 0)),
            pl.BlockSpec((tm, POOL_WIDTH), lambda i: (i, 0)),
            pl.BlockSpec((HALO, POOL_WIDTH), lambda i: (jnp.maximum(i * halo_blocks - 1, 0), 0)),
            pl.BlockSpec((tm, o_fox.shape[1]), lambda i: (i, 0)),
            pl.BlockSpec((tm, mq.shape[1]), lambda i: (i, 0)),
            mem_spec, mem_spec,
            _resident((None, POOL_WIDTH, POOL_WIDTH), lambda i: (layer, 0, 0)),
            pl.BlockSpec((None, 1, POOL_WIDTH), lambda i: (layer, 0, 0)),
            _resident((None, d, d), lambda i: (layer, 0, 0)),
        ],
        out_specs=pl.BlockSpec((tm, d), lambda i: (i, 0)),
        out_shape=jax.ShapeDtypeStruct((t, d), F32),
        scratch_shapes=[pltpu.VMEM((tm + HALO, POOL_WIDTH), F32)],
        compiler_params=_params(1),
        name="mix_prompt",
    )(x, u_pool, u_pool, o_fox, mq, mem_kt, mem_vt, wts["w_pool"], wts["pool_scale"], wts["w_out"])


def _mix_sample_body(x_ref, u_ref, st_ref, of_ref, om_ref, wp_ref, ps_ref, wo_ref, o_ref):
    u = u_ref[...]
    st = st_ref[...]
    row = lax.broadcasted_iota(jnp.int32, st.shape, 1)
    win3 = _pool_window_lanes(st.shape, 2)
    hist = jnp.sum(jnp.where(row >= POOL_HIST + 1 - win3, st, 0.0), axis=1)
    win = _pool_window_lanes(u.shape, 1).astype(F32)
    pooled = (hist + u) / win - u
    o_pool = _dot(pooled.astype(BF16), wp_ref[...]) * ps_ref[...]
    n_pool, n_fox = o_pool.shape[1], of_ref.shape[1]
    y = _dot(o_pool.astype(BF16), wo_ref[0:n_pool, :])
    y = y + _dot(of_ref[...].astype(BF16), wo_ref[n_pool:n_pool + n_fox, :])
    y = y + _dot(om_ref[...].astype(BF16), wo_ref[n_pool + n_fox:, :])
    o_ref[...] = x_ref[...] + y


def _mix_sample(x, u_pool, state_pool, o_fox, o_mem, wts, layer):
    t, d = x.shape
    full = lambda a: pl.BlockSpec(a.shape, lambda i: (0,) * a.ndim)
    return pl.pallas_call(
        _mix_sample_body,
        grid=(1,),
        in_specs=[
            full(x), full(u_pool),
            pl.BlockSpec((t, None) + state_pool.shape[2:], lambda i: (0, layer, 0, 0)),
            full(o_fox), full(o_mem),
            pl.BlockSpec((None, POOL_WIDTH, POOL_WIDTH), lambda i: (layer, 0, 0)),
            pl.BlockSpec((None, 1, POOL_WIDTH), lambda i: (layer, 0, 0)),
            pl.BlockSpec((None, d, d), lambda i: (layer, 0, 0)),
        ],
        out_specs=full(x),
        out_shape=jax.ShapeDtypeStruct((t, d), F32),
        compiler_params=_params(1),
        name="mix_sample",
    )(x, u_pool, state_pool, o_fox, o_mem, wts["w_pool"], wts["pool_scale"], wts["w_out"])


def _column_attend(q_t, kt, vt, bias, k_new_t, v_new_t, n_heads):
    lane = lax.broadcasted_iota(jnp.int32, q_t.shape, 1)
    out = jnp.zeros(q_t.shape, F32)
    for h in range(n_heads):
        qc = q_t[:, h:h + 1]
        s = jnp.sum(kt(h) * qc, axis=0, keepdims=True)
        if bias is not None:
            s = s + bias[h:h + 1, :]
        m = jnp.max(s, axis=1, keepdims=True)
        if k_new_t is not None:
            s_new = jnp.sum(qc * k_new_t[:, h:h + 1], axis=0, keepdims=True)
            m = jnp.maximum(m, s_new)
        p = jnp.exp(s - m)
        den = jnp.sum(p, axis=1, keepdims=True)
        o = jnp.sum(vt(h) * p, axis=1, keepdims=True)
        if k_new_t is not None:
            p_new = jnp.exp(s_new - m)
            den = den + p_new
            o = o + p_new * v_new_t[:, h:h + 1]
        out = jnp.where(lane == h, o / den, out)
    return out


def _decode_body(pt_ref, q_ref, ks_ref, vs_ref, lfs_ref, mq_ref, ck_hbm, cv_hbm, cf_hbm, mk_ref, mv_ref,
                 of_ref, om_ref, kbuf, vbuf, fbuf, sem, *, layer, n_pages, page, n_heads, n_mem_heads):
    b = pl.program_id(0)
    n_samples = pl.num_programs(0)
    slot = b % 2

    def page_copies(sample, sl):
        cps = []
        for j in range(n_pages):
            pg = pt_ref[sample, j]
            cols = pl.ds(j * page, page)
            cps.append(pltpu.make_async_copy(ck_hbm.at[pg, layer], kbuf.at[sl, :, :, cols], sem.at[sl, 0]))
            cps.append(pltpu.make_async_copy(cv_hbm.at[pg, layer], vbuf.at[sl, :, :, cols], sem.at[sl, 1]))
            cps.append(pltpu.make_async_copy(cf_hbm.at[pg, layer], fbuf.at[sl, :, cols], sem.at[sl, 2]))
        return cps

    @pl.when(b == 0)
    def _():
        for cp in page_copies(0, 0):
            cp.start()

    @pl.when(b + 1 < n_samples)
    def _():
        for cp in page_copies(b + 1, 1 - slot):
            cp.start()

    for cp in page_copies(b, slot):
        cp.wait()

    past = n_pages * page
    ft = fbuf[slot]
    lane = lax.broadcasted_iota(jnp.int32, ft.shape, 1)
    suffix = ft
    shift = 1
    while shift < past:
        suffix = suffix + jnp.where(lane + shift < past, pltpu.roll(suffix, past - shift, axis=1), 0.0)
        shift *= 2
    bias = (suffix - ft) + lfs_ref[...]

    o_t = _column_attend(q_ref[...].T, lambda h: kbuf[slot, h], lambda h: vbuf[slot, h], bias,
                         ks_ref[...].T, vs_ref[...].T, n_heads)
    of_ref[...] = o_t.T
    om_t = _column_attend(mq_ref[...].T, lambda h: mk_ref[h], lambda h: mv_ref[h], None, None, None, n_mem_heads)
    om_ref[...] = om_t.T


def _decode(page_table, q, k_new, v_new, lf_new, mq, cache_kt, cache_vt, cache_ft, cache_mem_kt, cache_mem_vt, layer):
    n_samples, n_pages = page_table.shape
    _, depth, n_heads, hd, page = cache_kt.shape
    _, _, n_mem_heads, _, m_tok = cache_mem_kt.shape
    past = n_pages * page
    per_sample = lambda rows, w: pl.BlockSpec((None, rows, w), lambda b, pt: (b, 0, 0))
    mem_spec = pl.BlockSpec((None, None, n_mem_heads, hd, m_tok), lambda b, pt: (b, layer, 0, 0, 0))
    any_spec = pl.BlockSpec(memory_space=pl.ANY)
    grid_spec = pltpu.PrefetchScalarGridSpec(
        num_scalar_prefetch=1,
        grid=(n_samples,),
        in_specs=[per_sample(n_heads, hd), per_sample(n_heads, hd), per_sample(n_heads, hd), per_sample(n_heads, 1),
                  per_sample(SUBLANES, hd), any_spec, any_spec, any_spec, mem_spec, mem_spec],
        out_specs=[per_sample(n_heads, hd), per_sample(SUBLANES, hd)],
        scratch_shapes=[pltpu.VMEM((2, n_heads, hd, past), F32),
                        pltpu.VMEM((2, n_heads, hd, past), F32),
                        pltpu.VMEM((2, n_heads, past), F32),
                        pltpu.SemaphoreType.DMA((2, 3))],
    )
    return pl.pallas_call(
        functools.partial(_decode_body, layer=layer, n_pages=n_pages, page=page, n_heads=n_heads,
                          n_mem_heads=n_mem_heads),
        grid_spec=grid_spec,
        out_shape=[jax.ShapeDtypeStruct((n_samples, n_heads, hd), F32),
                   jax.ShapeDtypeStruct((n_samples, SUBLANES, hd), F32)],
        compiler_params=_params(1),
        name="decode",
    )(page_table, q, k_new, v_new, lf_new, mq, cache_kt, cache_vt, cache_ft, cache_mem_kt, cache_mem_vt)


def _block_diag(blocks):
    n, r, c = blocks.shape
    eye = jnp.eye(n, dtype=blocks.dtype)
    return (eye[:, None, :, None] * blocks[:, :, None, :]).reshape(n * r, n * c)


def kernel(x_prompt, x_sample, cache_k, cache_v, cache_logf, state_pool, cache_mem_k, cache_mem_v, page_table, mem_prompt, g_ffn1, w_ffn1_gu, w_ffn1_down, g_mix, w_in, b_forget, g_fox_q, g_fox_k, w_pool, pool_scale, g_mem, w_mem_kv, g_mem_q, g_mem_k, w_out, g_ffn2, w_ffn2_gu, w_ffn2_down):
    n_seq, seq, d = x_prompt.shape
    n_samples = x_sample.shape[0]
    depth = w_in.shape[0]
    n_heads = cache_k.shape[3]
    n_mem_heads = cache_mem_k.shape[3]
    fox_w = n_heads * HEAD_DIM
    mem_w = n_mem_heads * HEAD_DIM
    assert x_sample.shape[1] == 1 and seq % TOKEN_TILE == 0 and seq % ATTN_BLOCK == 0

    o_q, o_k, o_v = POOL_WIDTH, POOL_WIDTH + fox_w, POOL_WIDTH + 2 * fox_w
    o_f = POOL_WIDTH + 3 * fox_w
    w_rows = jnp.concatenate(
        [w_in[:, :, :o_k], w_in[:, :, o_f + n_heads:], w_in[:, :, o_f:o_f + n_heads],
         jnp.zeros((depth, d, LANES - n_heads), w_in.dtype)], axis=2).astype(BF16)
    w_kv = w_in[:, :, o_k:o_f].astype(BF16)
    row3 = lambda a: a.reshape(depth, 1, -1)
    col3 = lambda a: a.reshape(depth, -1, 1)
    wts = {
        "g_mix": row3(g_mix),
        "w_rows": w_rows,
        "w_kv": w_kv,
        "w_kv_t": jnp.swapaxes(w_kv, 1, 2),
        "b_f": jnp.pad(b_forget, ((0, 0), (0, LANES - n_heads))).reshape(depth, 1, LANES),
        "g_q": row3(jnp.tile(g_fox_q, (1, n_heads))),
        "g_k": row3(jnp.tile(g_fox_k, (1, n_heads))),
        "g_k_col": col3(g_fox_k),
        "g_mq": row3(jnp.tile(g_mem_q, (1, n_mem_heads))),
        "bd": _block_diag(jnp.full((NORM_TILE // HEAD_DIM, HEAD_DIM, HEAD_DIM), 1.0 / HEAD_DIM, BF16)),
        "w_pool": jax.vmap(_block_diag)(w_pool).astype(BF16),
        "pool_scale": row3(pool_scale),
        "w_out": w_out.astype(BF16),
    }
    ffn1 = (row3(g_ffn1), w_ffn1_gu.astype(BF16), w_ffn1_down.astype(BF16))
    ffn2 = (row3(g_ffn2), w_ffn2_gu.astype(BF16), w_ffn2_down.astype(BF16))

    cache_kt = jnp.transpose(cache_k, (0, 1, 3, 4, 2))
    cache_vt = jnp.transpose(cache_v, (0, 1, 3, 4, 2))
    cache_ft = jnp.transpose(cache_logf, (0, 1, 3, 2))
    cache_mem_kt = jnp.transpose(cache_mem_k, (0, 1, 3, 4, 2))
    cache_mem_vt = jnp.transpose(cache_mem_v, (0, 1, 3, 4, 2))

    mem_kt, mem_vt = _mem_kv(mem_prompt, row3(g_mem), jnp.swapaxes(w_mem_kv, 1, 2).astype(BF16), col3(g_mem_k))

    xp = x_prompt.reshape(n_seq * seq, d)
    xs = x_sample.reshape(n_samples, d)
    kv_bufs = (jnp.zeros((n_seq, depth, fox_w, seq), F32), jnp.zeros((n_seq, depth, fox_w, seq), F32))
    fp_l, pp_l, ks_l, vs_l, fs_l, ps_l = [], [], [], [], [], []
    for layer in range(depth):
        xp = _ffn(xp, *ffn1, layer, TOKEN_TILE)
        u_pool, q, mq, kt_all, vt_all, ktb, vtb, lft, ct = _proj_prompt(xp, wts, layer, n_seq, seq, n_heads, kv_bufs)
        kv_bufs = (kt_all, vt_all)
        o_fox = _fox_prompt(q.reshape(n_seq, seq, fox_w), ktb, vtb, ct, n_heads)
        xp = _mix_prompt(xp, u_pool, o_fox.reshape(n_seq * seq, fox_w), mq, mem_kt, mem_vt, wts, layer, seq)
        xp = _ffn(xp, *ffn2, layer, TOKEN_TILE)
        fp_l.append(lft)
        pp_l.append(u_pool.reshape(n_seq, seq, POOL_WIDTH)[:, seq - POOL_HIST:])

        xs = _ffn(xs, *ffn1, layer, n_samples)
        u_s, q_s, mq_s, k_s, v_s, lf_s = _proj_sample(xs, wts, layer, n_heads)
        heads = lambda a, h: a.reshape(n_samples, h, HEAD_DIM)
        mq_pad = jnp.pad(heads(mq_s, n_mem_heads), ((0, 0), (0, SUBLANES - n_mem_heads), (0, 0)))
        o_fox_s, o_mem_s = _decode(page_table, heads(q_s, n_heads), heads(k_s, n_heads), heads(v_s, n_heads),
                                   lf_s.reshape(n_samples, n_heads, 1), mq_pad,
                                   cache_kt, cache_vt, cache_ft, cache_mem_kt, cache_mem_vt, layer)
        xs = _mix_sample(xs, u_s, state_pool, o_fox_s.reshape(n_samples, fox_w),
                         o_mem_s[:, :n_mem_heads].reshape(n_samples, mem_w), wts, layer)
        xs = _ffn(xs, *ffn2, layer, n_samples)
        ks_l.append(k_s.reshape(n_samples, 1, n_heads, HEAD_DIM))
        vs_l.append(v_s.reshape(n_samples, 1, n_heads, HEAD_DIM))
        fs_l.append(lf_s.reshape(n_samples, 1, n_heads))
        ps_l.append(jnp.concatenate([state_pool[:, layer, 1:], u_s[:, None, :]], axis=1))

    stack = lambda xs_: jnp.stack(xs_, axis=1)
    to_token_major = lambda a, h: jnp.transpose(a.reshape(a.shape[0], depth, h, HEAD_DIM, a.shape[3]), (0, 1, 4, 2, 3))
    return (xp.reshape(n_seq, seq, d), xs.reshape(n_samples, 1, d),
            to_token_major(kv_bufs[0], n_heads), to_token_major(kv_bufs[1], n_heads),
            jnp.transpose(stack(fp_l), (0, 1, 3, 2)), stack(pp_l),
            to_token_major(mem_kt, n_mem_heads), to_token_major(mem_vt, n_mem_heads),
            stack(ks_l), stack(vs_l), stack(fs_l), stack(ps_l))
```

```python
import functools

import jax
import jax.numpy as jnp
from jax import lax
from jax.experimental import pallas as pl
from jax.experimental.pallas import tpu as pltpu

F32 = jnp.float32
BF16 = jnp.bfloat16

HEAD_DIM = 64
POOL_WINDOWS = (2, 4, 8, 16)
POOL_GROUP = 64
POOL_WIDTH = len(POOL_WINDOWS) * POOL_GROUP
POOL_HIST = max(POOL_WINDOWS) - 1
EPS = 1e-6
SCALE = HEAD_DIM ** -0.5

LANES = 128
SUBLANES = 8
VMEM_LIMIT_BYTES = 56 * 1024 * 1024
NEG_BIG = -1e30

TOKEN_TILE = 512
FFN_CHUNK = 256
ATTN_BLOCK = 256
HALO = 16
NORM_TILE = 256


def _dot(a, b):
    return jnp.dot(a, b, preferred_element_type=F32)


def _dot_nt(a, b):
    return lax.dot_general(a, b, (((1,), (1,)), ((), ())), preferred_element_type=F32)


def _rms(x, g):
    return x * lax.rsqrt(jnp.mean(x * x, axis=-1, keepdims=True) + EPS) * g


def _params(n_axes):
    return pltpu.CompilerParams(dimension_semantics=("arbitrary",) * n_axes, vmem_limit_bytes=VMEM_LIMIT_BYTES)


def _resident(shape, index_map):
    return pl.BlockSpec(shape, index_map, pipeline_mode=pl.Buffered(1))


def _layer_block(shape, layer, resident=False):
    make = _resident if resident else pl.BlockSpec
    return make((None,) + shape, lambda *_: (layer,) + (0,) * len(shape))


def _head_norm(u, g, bd_ref):
    sq = (u * u).astype(BF16)
    ms = jnp.concatenate([_dot(sq[:, o:o + NORM_TILE], bd_ref[...]) for o in range(0, u.shape[1], NORM_TILE)], axis=1)
    return u * lax.rsqrt(ms + EPS) * g


def _head_norm_t(ut, g_col):
    rows, cols = ut.shape
    u3 = ut.reshape(rows // HEAD_DIM, HEAD_DIM, cols)
    ms = jnp.mean(u3 * u3, axis=1, keepdims=True)
    return (u3 * lax.rsqrt(ms + EPS) * g_col[None]).reshape(rows, cols)


def _ffn_value(x, g_ref, wgu_ref, wd_ref):
    d_ff = wd_ref.shape[0]
    xn = _rms(x, g_ref[...]).astype(BF16)
    acc = None
    for c in range(d_ff // FFN_CHUNK):
        lo = c * FFN_CHUNK
        gate = _dot(xn, wgu_ref[:, lo:lo + FFN_CHUNK])
        up = _dot(xn, wgu_ref[:, d_ff + lo:d_ff + lo + FFN_CHUNK])
        act = (gate * jax.nn.sigmoid(gate) * up).astype(BF16)
        part = _dot(act, wd_ref[lo:lo + FFN_CHUNK, :])
        acc = part if acc is None else acc + part
    return x + 0.5 * acc


def _ffn_specs(ffn, layer, resident):
    g, wgu, wd = ffn
    return [_layer_block(g.shape[1:], layer), _layer_block(wgu.shape[1:], layer, resident),
            _layer_block(wd.shape[1:], layer, resident)]


def _proj_rows(x, g_ref, w_ref, bf_ref, gq_ref, gmq_ref, bd_ref):
    xn = _rms(x, g_ref[...]).astype(BF16)
    u = _dot(xn, w_ref[...])
    n_q = gq_ref.shape[1]
    o_q, o_mq = POOL_WIDTH, POOL_WIDTH + n_q
    o_f = o_mq + gmq_ref.shape[1]
    u_pool = u[:, :o_q]
    q = _head_norm(u[:, o_q:o_mq], gq_ref[...], bd_ref) * SCALE
    mq = _head_norm(u[:, o_mq:o_f], gmq_ref[...], bd_ref) * SCALE
    logf = jax.nn.log_sigmoid(u[:, o_f:] + bf_ref[...])
    return xn, u_pool, q, mq, logf


def _ffn_proj_prompt_body(x_ref, g1_ref, wgu_ref, wd_ref, g_ref, w_ref, wkv_ref, bf_ref, gq_ref, gmq_ref, gk_ref, bd_ref,
                          kt_in, vt_in,
                          xo_ref, up_ref, q_ref, mq_ref, kt_ref, vt_ref, ktb_ref, vtb_ref, lft_ref, ct_ref, carry_ref,
                          *, tm, tiles_per_seq, n_heads):
    del kt_in, vt_in
    x = _ffn_value(x_ref[...], g1_ref, wgu_ref, wd_ref)
    xo_ref[...] = x
    xn, u_pool, q, mq, logf = _proj_rows(x, g_ref, w_ref, bf_ref, gq_ref, gmq_ref, bd_ref)
    up_ref[...] = u_pool
    q_ref[...] = q.astype(BF16)
    mq_ref[...] = mq.astype(BF16)

    kvt = _dot_nt(wkv_ref[...], xn)
    half = kvt.shape[0] // 2
    kt = _head_norm_t(kvt[:half], gk_ref[...])
    vt = kvt[half:]
    kt_ref[...] = kt
    vt_ref[...] = vt
    ktb_ref[...] = kt.astype(BF16)
    vtb_ref[...] = vt.astype(BF16)

    @pl.when(pl.program_id(0) % tiles_per_seq == 0)
    def _():
        carry_ref[...] = jnp.zeros_like(carry_ref)

    ft = logf.T[:n_heads]
    lft_ref[...] = ft
    lane = lax.broadcasted_iota(jnp.int32, ft.shape, 1)
    shift = 1
    while shift < tm:
        ft = ft + jnp.where(lane >= shift, pltpu.roll(ft, shift, axis=1), 0.0)
        shift *= 2
    ct = ft + jnp.concatenate([carry_ref[...]] * (tm // LANES), axis=1)
    carry_ref[...] = jnp.broadcast_to(ct[:, tm - 1:tm], carry_ref.shape)
    ct_ref[...] = ct


def _ffn_proj_prompt(x, ffn, wts, layer, n_seq, seq, n_heads, kv_bufs):
    t, d = x.shape
    tm = TOKEN_TILE
    tps = seq // tm
    depth, kv_rows, _ = wts["w_kv_t"].shape
    width = kv_rows // 2
    row = lambda w, dt: (pl.BlockSpec((tm, w), lambda i: (i, 0)), jax.ShapeDtypeStruct((t, w), dt))
    feat = lambda dt: (pl.BlockSpec((None, width, tm), lambda i: (i // tps, 0, i % tps)),
                       jax.ShapeDtypeStruct((n_seq, width, seq), dt))
    stacked = (pl.BlockSpec((None, None, width, tm), lambda i: (i // tps, layer, 0, i % tps)),
               jax.ShapeDtypeStruct((n_seq, depth, width, seq), F32))
    head_rows = (pl.BlockSpec((None, n_heads, tm), lambda i: (i // tps, 0, i % tps)),
                 jax.ShapeDtypeStruct((n_seq, n_heads, seq), F32))
    outs = [row(d, F32), row(POOL_WIDTH, F32), row(width, BF16), row(wts["g_mq"].shape[2], BF16),
            stacked, stacked, feat(BF16), feat(BF16), head_rows, head_rows]
    proj_names = ("g_mix", "w_rows", "w_kv_t", "b_f", "g_q", "g_mq", "g_k_col")
    in_specs = ([pl.BlockSpec((tm, d), lambda i: (i, 0))] + _ffn_specs(ffn, layer, True)
                + [_layer_block(wts[n].shape[1:], layer, n.startswith("w_")) for n in proj_names]
                + [_resident((NORM_TILE, NORM_TILE), lambda i: (0, 0)),
                   pl.BlockSpec(memory_space=pl.ANY), pl.BlockSpec(memory_space=pl.ANY)])
    n_in = len(in_specs)
    return pl.pallas_call(
        functools.partial(_ffn_proj_prompt_body, tm=tm, tiles_per_seq=tps, n_heads=n_heads),
        grid=(t // tm,),
        in_specs=in_specs,
        out_specs=[o[0] for o in outs],
        out_shape=[o[1] for o in outs],
        scratch_shapes=[pltpu.VMEM((n_heads, LANES), F32)],
        input_output_aliases={n_in - 2: 4, n_in - 1: 5},
        compiler_params=_params(1),
        name="ffn_proj_prompt",
    )(x, *ffn, *[wts[n] for n in proj_names], wts["bd"], *kv_bufs)


def _ffn_proj_sample_body(x_ref, g1_ref, wgu_ref, wd_ref, g_ref, w_ref, wkv_ref, bf_ref, gq_ref, gmq_ref, gk_ref, bd_ref,
                          xo_ref, up_ref, q_ref, mq_ref, k_ref, v_ref, lf_ref, *, n_heads):
    x = _ffn_value(x_ref[...], g1_ref, wgu_ref, wd_ref)
    xo_ref[...] = x
    xn, u_pool, q, mq, logf = _proj_rows(x, g_ref, w_ref, bf_ref, gq_ref, gmq_ref, bd_ref)
    up_ref[...] = u_pool
    q_ref[...] = q
    mq_ref[...] = mq
    lf_ref[...] = logf[:, :n_heads]
    kv = _dot(xn, wkv_ref[...])
    half = kv.shape[1] // 2
    k_ref[...] = _head_norm(kv[:, :half], gk_ref[...], bd_ref)
    v_ref[...] = kv[:, half:]


def _ffn_proj_sample(x, ffn, wts, layer, n_heads):
    t, d = x.shape
    width = wts["g_q"].shape[2]
    mem_w = wts["g_mq"].shape[2]
    row = lambda w: (pl.BlockSpec((t, w), lambda i: (0, 0)), jax.ShapeDtypeStruct((t, w), F32))
    outs = [row(d), row(POOL_WIDTH), row(width), row(mem_w), row(width), row(width), row(n_heads)]
    proj_names = ("g_mix", "w_rows", "w_kv", "b_f", "g_q", "g_mq", "g_k")
    return pl.pallas_call(
        functools.partial(_ffn_proj_sample_body, n_heads=n_heads),
        grid=(1,),
        in_specs=([pl.BlockSpec((t, d), lambda i: (0, 0))] + _ffn_specs(ffn, layer, False)
                  + [_layer_block(wts[n].shape[1:], layer) for n in proj_names]
                  + [pl.BlockSpec((NORM_TILE, NORM_TILE), lambda i: (0, 0))]),
        out_specs=[o[0] for o in outs],
        out_shape=[o[1] for o in outs],
        compiler_params=_params(1),
        name="ffn_proj_sample",
    )(x, *ffn, *[wts[n] for n in proj_names], wts["bd"])


def _fox_body(q_ref, kt_ref, vt_ref, ct_ref, o_ref, *, blk, n_blk):
    pair = pl.program_id(1)
    lane = lax.broadcasted_iota(jnp.int32, (blk, LANES), 1)
    causal = lax.broadcasted_iota(jnp.int32, (blk, blk), 0) >= lax.broadcasted_iota(jnp.int32, (blk, blk), 1)
    ck_rows = [ct_ref[pl.ds(2 * pair + e, 1), :] for e in range(2)]

    for i in range(n_blk):
        n0, n1 = i * blk, (i + 1) * blk
        q2 = q_ref[n0:n1, :]
        zero = jnp.zeros_like(q2)
        outs = []
        for e in range(2):
            qe = jnp.where((lane >= HEAD_DIM) == bool(e), q2, zero)
            ck = ck_rows[e][:, :n1] - ck_rows[e][:, n0:n0 + 1]
            s_d = jnp.where(causal, _dot(qe, kt_ref[:, n0:n1]) - ck[:, n0:n1], NEG_BIG)
            m = jnp.max(s_d, axis=1, keepdims=True)
            if i:
                s_f = _dot(qe, kt_ref[:, :n0]) - ck[:, :n0]
                m = jnp.maximum(m, jnp.max(s_f, axis=1, keepdims=True))
            p_d = jnp.exp(s_d - m)
            den = jnp.sum(p_d, axis=1, keepdims=True)
            o = _dot_nt(p_d.astype(BF16), vt_ref[:, n0:n1])
            if i:
                p_f = jnp.exp(s_f - m)
                den = den + jnp.sum(p_f, axis=1, keepdims=True)
                o = o + _dot_nt(p_f.astype(BF16), vt_ref[:, :n0])
            outs.append(o / den)
        o_ref[n0:n1, :] = jnp.where(lane < HEAD_DIM, outs[0], outs[1]).astype(o_ref.dtype)


def _fox_prompt(q, ktb, vtb, ct, n_heads):
    n_seq, seq, width = q.shape
    blk = ATTN_BLOCK
    rows = pl.BlockSpec((None, seq, LANES), lambda b, p: (b, 0, p))
    feats = pl.BlockSpec((None, LANES, seq), lambda b, p: (b, p, 0))
    return pl.pallas_call(
        functools.partial(_fox_body, blk=blk, n_blk=seq // blk),
        grid=(n_seq, width // LANES),
        in_specs=[rows, feats, feats, pl.BlockSpec((None, n_heads, seq), lambda b, p: (b, 0, 0))],
        out_specs=rows,
        out_shape=jax.ShapeDtypeStruct((n_seq, seq, width), BF16),
        compiler_params=_params(2),
        name="fox_prompt",
    )(q, ktb, vtb, ct)


def _memkv_body(mem_ref, g_ref, w_ref, gk_ref, mk_ref, mv_ref):
    xn = _rms(mem_ref[...], g_ref[...]).astype(BF16)
    kvt = _dot_nt(w_ref[...], xn)
    half = kvt.shape[0] // 2
    mk_ref[...] = _head_norm_t(kvt[:half], gk_ref[...])
    mv_ref[...] = kvt[half:]


def _mem_kv(mem, g_mem, w_kv_t, g_k_col):
    n_seq, m_tok, d = mem.shape
    depth, two_w, _ = w_kv_t.shape
    w = two_w // 2
    out = (pl.BlockSpec((None, None, w, m_tok), lambda l, b: (b, l, 0, 0)),
           jax.ShapeDtypeStruct((n_seq, depth, w, m_tok), F32))
    return pl.pallas_call(
        _memkv_body,
        grid=(depth, n_seq),
        in_specs=[pl.BlockSpec((None, m_tok, d), lambda l, b: (b, 0, 0)),
                  pl.BlockSpec((None, 1, d), lambda l, b: (l, 0, 0)),
                  pl.BlockSpec((None, two_w, d), lambda l, b: (l, 0, 0)),
                  pl.BlockSpec((None, HEAD_DIM, 1), lambda l, b: (l, 0, 0))],
        out_specs=[out[0], out[0]],
        out_shape=[out[1], out[1]],
        compiler_params=_params(2),
        name="mem_kv",
    )(mem, g_mem, w_kv_t, g_k_col)


def _pool_window_lanes(shape, axis):
    lane = lax.broadcasted_iota(jnp.int32, shape, axis)
    w = jnp.full(shape, POOL_WINDOWS[-1], jnp.int32)
    for g in range(len(POOL_WINDOWS) - 2, -1, -1):
        w = jnp.where(lane < (g + 1) * POOL_GROUP, POOL_WINDOWS[g], w)
    return w


def _pair_attend(q2, kt2, vt2):
    lane = lax.broadcasted_iota(jnp.int32, q2.shape, 1)
    zero = jnp.zeros_like(q2)
    outs = []
    for e in range(2):
        qe = jnp.where((lane >= HEAD_DIM) == bool(e), q2, zero)
        s = _dot(qe, kt2)
        p = jnp.exp(s - jnp.max(s, axis=1, keepdims=True))
        outs.append(_dot_nt(p.astype(BF16), vt2) / jnp.sum(p, axis=1, keepdims=True))
    return jnp.where(lane < HEAD_DIM, outs[0], outs[1])


def _mix_out(x, o_pool, o_fox, o_mem, wo_ref):
    n_pool, n_fox = o_pool.shape[1], o_fox.shape[1]
    y = _dot(o_pool.astype(BF16), wo_ref[0:n_pool, :])
    y = y + _dot(o_fox.astype(BF16), wo_ref[n_pool:n_pool + n_fox, :])
    y = y + _dot(o_mem.astype(BF16), wo_ref[n_pool + n_fox:, :])
    return x + y


def _mix_ffn_prompt_body(x_ref, up_ref, upp_ref, of_ref, mq_ref, mk_ref, mv_ref, wp_ref, ps_ref, wo_ref,
                         g2_ref, wgu_ref, wd_ref, o_ref, z_ref, *, tm, tiles_per_seq):
    tile = pl.program_id(0) % tiles_per_seq
    z = up_ref[...]
    z_ref[0:HALO, :] = jnp.where(tile == 0, 0.0, upp_ref[...])
    z_ref[HALO:, :] = z
    win = _pool_window_lanes(z.shape, 1)
    run = z
    wsum = jnp.zeros_like(z)
    for s in range(1, POOL_WINDOWS[-1]):
        run = run + z_ref[pl.ds(HALO - s, tm), :]
        if s + 1 in POOL_WINDOWS:
            wsum = jnp.where(win == s + 1, run, wsum)
    pos = tile * tm + lax.broadcasted_iota(jnp.int32, z.shape, 0)
    cnt = jnp.minimum(pos + 1, win).astype(F32)
    pooled = wsum / cnt - z
    o_pool = _dot(pooled.astype(BF16), wp_ref[...]) * ps_ref[...]

    o_mem = jnp.concatenate(
        [_pair_attend(mq_ref[:, o:o + LANES], mk_ref[o:o + LANES, :].astype(BF16), mv_ref[o:o + LANES, :].astype(BF16))
         for o in range(0, mq_ref.shape[1], LANES)], axis=1)

    x = _mix_out(x_ref[...], o_pool, of_ref[...], o_mem, wo_ref)
    o_ref[...] = _ffn_value(x, g2_ref, wgu_ref, wd_ref)


def _mix_ffn_prompt(x, u_pool, o_fox, mq, mem_kt, mem_vt, ffn, wts, layer, seq):
    t, d = x.shape
    tm = TOKEN_TILE
    tps = seq // tm
    halo_blocks = tm // HALO
    mem_spec = pl.BlockSpec((None, None) + mem_kt.shape[2:], lambda i: (i // tps, layer, 0, 0))
    return pl.pallas_call(
        functools.partial(_mix_ffn_prompt_body, tm=tm, tiles_per_seq=tps),
        grid=(t // tm,),
        in_specs=[
            pl.BlockSpec((tm, d), lambda i: (i, 0)),
            pl.BlockSpec((tm, POOL_WIDTH), lambda i: (i, 0)),
            pl.BlockSpec((HALO, POOL_WIDTH), lambda i: (jnp.maximum(i * halo_blocks - 1, 0), 0)),
            pl.BlockSpec((tm, o_fox.shape[1]), lambda i: (i, 0)),
            pl.BlockSpec((tm, mq.shape[1]), lambda i: (i, 0)),
            mem_spec, mem_spec,
            _layer_block(wts["w_pool"].shape[1:], layer, True),
            _layer_block(wts["pool_scale"].shape[1:], layer),
            _layer_block(wts["w_out"].shape[1:], layer, True),
        ] + _ffn_specs(ffn, layer, True),
        out_specs=pl.BlockSpec((tm, d), lambda i: (i, 0)),
        out_shape=jax.ShapeDtypeStruct((t, d), F32),
        scratch_shapes=[pltpu.VMEM((tm + HALO, POOL_WIDTH), F32)],
        compiler_params=_params(1),
        name="mix_ffn_prompt",
    )(x, u_pool, u_pool, o_fox, mq, mem_kt, mem_vt, wts["w_pool"], wts["pool_scale"], wts["w_out"], *ffn)


def _mix_ffn_sample_body(x_ref, u_ref, st_ref, of_ref, om_ref, wp_ref, ps_ref, wo_ref, g2_ref, wgu_ref, wd_ref, o_ref):
    u = u_ref[...]
    st = st_ref[...]
    row = lax.broadcasted_iota(jnp.int32, st.shape, 1)
    win3 = _pool_window_lanes(st.shape, 2)
    hist = jnp.sum(jnp.where(row >= POOL_HIST + 1 - win3, st, 0.0), axis=1)
    win = _pool_window_lanes(u.shape, 1).astype(F32)
    pooled = (hist + u) / win - u
    o_pool = _dot(pooled.astype(BF16), wp_ref[...]) * ps_ref[...]
    x = _mix_out(x_ref[...], o_pool, of_ref[...], om_ref[...], wo_ref)
    o_ref[...] = _ffn_value(x, g2_ref, wgu_ref, wd_ref)


def _mix_ffn_sample(x, u_pool, state_pool, o_fox, o_mem, ffn, wts, layer):
    t, d = x.shape
    full = lambda a: pl.BlockSpec(a.shape, lambda i: (0,) * a.ndim)
    return pl.pallas_call(
        _mix_ffn_sample_body,
        grid=(1,),
        in_specs=[
            full(x), full(u_pool),
            pl.BlockSpec((t, None) + state_pool.shape[2:], lambda i: (0, layer, 0, 0)),
            full(o_fox), full(o_mem),
            _layer_block(wts["w_pool"].shape[1:], layer),
            _layer_block(wts["pool_scale"].shape[1:], layer),
            _layer_block(wts["w_out"].shape[1:], layer),
        ] + _ffn_specs(ffn, layer, False),
        out_specs=full(x),
        out_shape=jax.ShapeDtypeStruct((t, d), F32),
        compiler_params=_params(1),
        name="mix_ffn_sample",
    )(x, u_pool, state_pool, o_fox, o_mem, wts["w_pool"], wts["pool_scale"], wts["w_out"], *ffn)


def _column_attend(q_rows, kt, vt, bias, k_new_t, v_new_t, n_heads):
    q_t = q_rows.T
    lane = lax.broadcasted_iota(jnp.int32, q_t.shape, 1)
    out = jnp.zeros(q_t.shape, F32)
    for h in range(n_heads):
        s = _dot(q_rows, kt(h))[h:h + 1, :]
        if bias is not None:
            s = s + bias[h:h + 1, :]
        m = jnp.max(s, axis=1, keepdims=True)
        if k_new_t is not None:
            s_new = jnp.sum(q_t[:, h:h + 1] * k_new_t[:, h:h + 1], axis=0, keepdims=True)
            m = jnp.maximum(m, s_new)
        p = jnp.exp(s - m)
        den = jnp.sum(p, axis=1, keepdims=True)
        o = jnp.sum(vt(h) * p, axis=1, keepdims=True)
        if k_new_t is not None:
            p_new = jnp.exp(s_new - m)
            den = den + p_new
            o = o + p_new * v_new_t[:, h:h + 1]
        out = jnp.where(lane == h, o / den, out)
    return out


def _decode_body(pt_ref, q_ref, ks_ref, vs_ref, lfs_ref, mq_ref, ck_hbm, cv_hbm, cf_hbm, mk_ref, mv_ref,
                 of_ref, om_ref, kbuf, vbuf, fbuf, sem, *, layer, n_pages, page, n_heads, n_mem_heads):
    b = pl.program_id(0)
    n_samples = pl.num_programs(0)
    slot = b % 2

    def page_copies(sample, sl):
        cps = []
        for j in range(n_pages):
            pg = pt_ref[sample, j]
            cols = pl.ds(j * page, page)
            cps.append(pltpu.make_async_copy(ck_hbm.at[pg, layer], kbuf.at[sl, :, :, cols], sem.at[sl, 0]))
            cps.append(pltpu.make_async_copy(cv_hbm.at[pg, layer], vbuf.at[sl, :, :, cols], sem.at[sl, 1]))
            cps.append(pltpu.make_async_copy(cf_hbm.at[pg, layer], fbuf.at[sl, :, cols], sem.at[sl, 2]))
        return cps

    @pl.when(b == 0)
    def _():
        for cp in page_copies(0, 0):
            cp.start()

    @pl.when(b + 1 < n_samples)
    def _():
        for cp in page_copies(b + 1, 1 - slot):
            cp.start()

    for cp in page_copies(b, slot):
        cp.wait()

    past = n_pages * page
    ft = fbuf[slot]
    lane = lax.broadcasted_iota(jnp.int32, ft.shape, 1)
    suffix = ft
    shift = 1
    while shift < past:
        suffix = suffix + jnp.where(lane + shift < past, pltpu.roll(suffix, past - shift, axis=1), 0.0)
        shift *= 2
    bias = (suffix - ft) + lfs_ref[...]

    o_t = _column_attend(q_ref[...], lambda h: kbuf[slot, h], lambda h: vbuf[slot, h], bias,
                         ks_ref[...].T, vs_ref[...].T, n_heads)
    of_ref[...] = o_t.T
    om_t = _column_attend(mq_ref[...], lambda h: mk_ref[h], lambda h: mv_ref[h], None, None, None, n_mem_heads)
    om_ref[...] = om_t.T


def _decode(page_table, q, k_new, v_new, lf_new, mq, cache_kt, cache_vt, cache_ft, cache_mem_kt, cache_mem_vt, layer):
    n_samples, n_pages = page_table.shape
    _, depth, n_heads, hd, page = cache_kt.shape
    _, _, n_mem_heads, _, m_tok = cache_mem_kt.shape
    past = n_pages * page
    per_sample = lambda rows, w: pl.BlockSpec((None, rows, w), lambda b, pt: (b, 0, 0))
    mem_spec = pl.BlockSpec((None, None, n_mem_heads, hd, m_tok), lambda b, pt: (b, layer, 0, 0, 0))
    any_spec = pl.BlockSpec(memory_space=pl.ANY)
    grid_spec = pltpu.PrefetchScalarGridSpec(
        num_scalar_prefetch=1,
        grid=(n_samples,),
        in_specs=[per_sample(n_heads, hd), per_sample(n_heads, hd), per_sample(n_heads, hd), per_sample(n_heads, 1),
                  per_sample(SUBLANES, hd), any_spec, any_spec, any_spec, mem_spec, mem_spec],
        out_specs=[per_sample(n_heads, hd), per_sample(SUBLANES, hd)],
        scratch_shapes=[pltpu.VMEM((2, n_heads, hd, past), F32),
                        pltpu.VMEM((2, n_heads, hd, past), F32),
                        pltpu.VMEM((2, n_heads, past), F32),
                        pltpu.SemaphoreType.DMA((2, 3))],
    )
    return pl.pallas_call(
        functools.partial(_decode_body, layer=layer, n_pages=n_pages, page=page, n_heads=n_heads,
                          n_mem_heads=n_mem_heads),
        grid_spec=grid_spec,
        out_shape=[jax.ShapeDtypeStruct((n_samples, n_heads, hd), F32),
                   jax.ShapeDtypeStruct((n_samples, SUBLANES, hd), F32)],
        compiler_params=_params(1),
        name="decode",
    )(page_table, q, k_new, v_new, lf_new, mq, cache_kt, cache_vt, cache_ft, cache_mem_kt, cache_mem_vt)


def _block_diag(blocks):
    n, r, c = blocks.shape
    eye = jnp.eye(n, dtype=blocks.dtype)
    return (eye[:, None, :, None] * blocks[:, :, None, :]).reshape(n * r, n * c)


def kernel(x_prompt, x_sample, cache_k, cache_v, cache_logf, state_pool, cache_mem_k, cache_mem_v, page_table, mem_prompt, g_ffn1, w_ffn1_gu, w_ffn1_down, g_mix, w_in, b_forget, g_fox_q, g_fox_k, w_pool, pool_scale, g_mem, w_mem_kv, g_mem_q, g_mem_k, w_out, g_ffn2, w_ffn2_gu, w_ffn2_down):
    n_seq, seq, d = x_prompt.shape
    n_samples = x_sample.shape[0]
    depth = w_in.shape[0]
    n_heads = cache_k.shape[3]
    n_mem_heads = cache_mem_k.shape[3]
    fox_w = n_heads * HEAD_DIM
    mem_w = n_mem_heads * HEAD_DIM
    assert x_sample.shape[1] == 1 and seq % TOKEN_TILE == 0 and seq % ATTN_BLOCK == 0

    o_k = POOL_WIDTH + fox_w
    o_f = POOL_WIDTH + 3 * fox_w
    w_rows = jnp.concatenate(
        [w_in[:, :, :o_k], w_in[:, :, o_f + n_heads:], w_in[:, :, o_f:o_f + n_heads],
         jnp.zeros((depth, d, LANES - n_heads), w_in.dtype)], axis=2).astype(BF16)
    w_kv = w_in[:, :, o_k:o_f].astype(BF16)
    row3 = lambda a: a.reshape(depth, 1, -1)
    col3 = lambda a: a.reshape(depth, -1, 1)
    wts = {
        "g_mix": row3(g_mix),
        "w_rows": w_rows,
        "w_kv": w_kv,
        "w_kv_t": jnp.swapaxes(w_kv, 1, 2),
        "b_f": jnp.pad(b_forget, ((0, 0), (0, LANES - n_heads))).reshape(depth, 1, LANES),
        "g_q": row3(jnp.tile(g_fox_q, (1, n_heads))),
        "g_k": row3(jnp.tile(g_fox_k, (1, n_heads))),
        "g_k_col": col3(g_fox_k),
        "g_mq": row3(jnp.tile(g_mem_q, (1, n_mem_heads))),
        "bd": _block_diag(jnp.full((NORM_TILE // HEAD_DIM, HEAD_DIM, HEAD_DIM), 1.0 / HEAD_DIM, BF16)),
        "w_pool": jax.vmap(_block_diag)(w_pool).astype(BF16),
        "pool_scale": row3(pool_scale),
        "w_out": w_out.astype(BF16),
    }
    ffn1 = (row3(g_ffn1), w_ffn1_gu.astype(BF16), w_ffn1_down.astype(BF16))
    ffn2 = (row3(g_ffn2), w_ffn2_gu.astype(BF16), w_ffn2_down.astype(BF16))

    cache_kt = jnp.transpose(cache_k, (0, 1, 3, 4, 2))
    cache_vt = jnp.transpose(cache_v, (0, 1, 3, 4, 2))
    cache_ft = jnp.transpose(cache_logf, (0, 1, 3, 2))
    cache_mem_kt = jnp.transpose(cache_mem_k, (0, 1, 3, 4, 2))
    cache_mem_vt = jnp.transpose(cache_mem_v, (0, 1, 3, 4, 2))

    mem_kt, mem_vt = _mem_kv(mem_prompt, row3(g_mem), jnp.swapaxes(w_mem_kv, 1, 2).astype(BF16), col3(g_mem_k))

    xp = x_prompt.reshape(n_seq * seq, d)
    xs = x_sample.reshape(n_samples, d)
    kv_bufs = (jnp.zeros((n_seq, depth, fox_w, seq), F32), jnp.zeros((n_seq, depth, fox_w, seq), F32))
    fp_l, pp_l, ks_l, vs_l, fs_l, ps_l = [], [], [], [], [], []
    for layer in range(depth):
        xp, u_pool, q, mq, kt_all, vt_all, ktb, vtb, lft, ct = _ffn_proj_prompt(
            xp, ffn1, wts, layer, n_seq, seq, n_heads, kv_bufs)
        kv_bufs = (kt_all, vt_all)
        o_fox = _fox_prompt(q.reshape(n_seq, seq, fox_w), ktb, vtb, ct, n_heads)
        xp = _mix_ffn_prompt(xp, u_pool, o_fox.reshape(n_seq * seq, fox_w), mq, mem_kt, mem_vt, ffn2, wts, layer, seq)
        fp_l.append(lft)
        pp_l.append(u_pool.reshape(n_seq, seq, POOL_WIDTH)[:, seq - POOL_HIST:])

        xs, u_s, q_s, mq_s, k_s, v_s, lf_s = _ffn_proj_sample(xs, ffn1, wts, layer, n_heads)
        heads = lambda a, h: a.reshape(n_samples, h, HEAD_DIM)
        mq_pad = jnp.pad(heads(mq_s, n_mem_heads), ((0, 0), (0, SUBLANES - n_mem_heads), (0, 0)))
        o_fox_s, o_mem_s = _decode(page_table, heads(q_s, n_heads), heads(k_s, n_heads), heads(v_s, n_heads),
                                   lf_s.reshape(n_samples, n_heads, 1), mq_pad,
                                   cache_kt, cache_vt, cache_ft, cache_mem_kt, cache_mem_vt, layer)
        xs = _mix_ffn_sample(xs, u_s, state_pool, o_fox_s.reshape(n_samples, fox_w),
                             o_mem_s[:, :n_mem_heads].reshape(n_samples, mem_w), ffn2, wts, layer)
        ks_l.append(k_s.reshape(n_samples, 1, n_heads, HEAD_DIM))
        vs_l.append(v_s.reshape(n_samples, 1, n_heads, HEAD_DIM))
        fs_l.append(lf_s.reshape(n_samples, 1, n_heads))
        ps_l.append(jnp.concatenate([state_pool[:, layer, 1:], u_s[:, None, :]], axis=1))

    stack = lambda xs_: jnp.stack(xs_, axis=1)
    to_token_major = lambda a, h: jnp.transpose(a.reshape(a.shape[0], depth, h, HEAD_DIM, a.shape[3]), (0, 1, 4, 2, 3))
    return (xp.reshape(n_seq, seq, d), xs.reshape(n_samples, 1, d),
            to_token_major(kv_bufs[0], n_heads), to_token_major(kv_bufs[1], n_heads),
            jnp.transpose(stack(fp_l), (0, 1, 3, 2)), stack(pp_l),
            to_token_major(mem_kt, n_mem_heads), to_token_major(mem_vt, n_mem_heads),
            stack(ks_l), stack(vs_l), stack(fs_l), stack(ps_l))
```

```python
import functools

import jax
import jax.numpy as jnp
from jax import lax
from jax.experimental import pallas as pl
from jax.experimental.pallas import tpu as pltpu

F32 = jnp.float32
BF16 = jnp.bfloat16

HEAD_DIM = 64
POOL_WINDOWS = (2, 4, 8, 16)
POOL_GROUP = 64
POOL_WIDTH = len(POOL_WINDOWS) * POOL_GROUP
POOL_HIST = max(POOL_WINDOWS) - 1
EPS = 1e-6
SCALE = HEAD_DIM ** -0.5

LANES = 128
SUBLANES = 8
VMEM_LIMIT_BYTES = 56 * 1024 * 1024
NEG_BIG = -1e30

TOKEN_TILE = 512
FOX_PAIRS = 4
FOX_SPLITS = ((0, 1, 2, 3, 4), (5, 6, 7))
FFN_CHUNK = 256
ATTN_BLOCK = 256
HALO = 16
NORM_TILE = 256


def _dot(a, b):
    return jnp.dot(a, b, preferred_element_type=F32)


def _dot_nt(a, b):
    return lax.dot_general(a, b, (((1,), (1,)), ((), ())), preferred_element_type=F32)


def _rms(x, g):
    return x * lax.rsqrt(jnp.mean(x * x, axis=-1, keepdims=True) + EPS) * g


def _params(n_axes):
    return pltpu.CompilerParams(dimension_semantics=("arbitrary",) * n_axes, vmem_limit_bytes=VMEM_LIMIT_BYTES)


def _resident(shape, index_map):
    return pl.BlockSpec(shape, index_map, pipeline_mode=pl.Buffered(1))


def _layer_block(shape, layer, resident=False):
    make = _resident if resident else pl.BlockSpec
    return make((None,) + shape, lambda *_: (layer,) + (0,) * len(shape))


def _head_norm(u, g, bd_ref):
    sq = (u * u).astype(BF16)
    ms = jnp.concatenate([_dot(sq[:, o:o + NORM_TILE], bd_ref[...]) for o in range(0, u.shape[1], NORM_TILE)], axis=1)
    return u * lax.rsqrt(ms + EPS) * g


def _head_norm_t(ut, g_col):
    rows, cols = ut.shape
    u3 = ut.reshape(rows // HEAD_DIM, HEAD_DIM, cols)
    ms = jnp.mean(u3 * u3, axis=1, keepdims=True)
    return (u3 * lax.rsqrt(ms + EPS) * g_col[None]).reshape(rows, cols)


def _ffn_value(x, g_ref, wgu_ref, wd_ref):
    d_ff = wd_ref.shape[0]
    xn = _rms(x, g_ref[...]).astype(BF16)
    acc = None
    for c in range(d_ff // FFN_CHUNK):
        lo = c * FFN_CHUNK
        gate = _dot(xn, wgu_ref[:, lo:lo + FFN_CHUNK])
        up = _dot(xn, wgu_ref[:, d_ff + lo:d_ff + lo + FFN_CHUNK])
        act = (gate * jax.nn.sigmoid(gate) * up).astype(BF16)
        part = _dot(act, wd_ref[lo:lo + FFN_CHUNK, :])
        acc = part if acc is None else acc + part
    return x + 0.5 * acc


def _ffn_specs(ffn, layer, resident):
    g, wgu, wd = ffn
    return [_layer_block(g.shape[1:], layer), _layer_block(wgu.shape[1:], layer, resident),
            _layer_block(wd.shape[1:], layer, resident)]


def _proj_rows(x, g_ref, w_ref, bf_ref, gq_ref, gmq_ref, bd_ref):
    xn = _rms(x, g_ref[...]).astype(BF16)
    u = _dot(xn, w_ref[...])
    n_q = gq_ref.shape[1]
    o_q, o_mq = POOL_WIDTH, POOL_WIDTH + n_q
    o_f = o_mq + gmq_ref.shape[1]
    u_pool = u[:, :o_q]
    q = _head_norm(u[:, o_q:o_mq], gq_ref[...], bd_ref) * SCALE
    mq = _head_norm(u[:, o_mq:o_f], gmq_ref[...], bd_ref) * SCALE
    logf = jax.nn.log_sigmoid(u[:, o_f:] + bf_ref[...])
    return xn, u_pool, q, mq, logf


def _ffn_proj_prompt_body(x_ref, g1_ref, wgu_ref, wd_ref, g_ref, w_ref, wkv_ref, bf_ref, gq_ref, gmq_ref, gk_ref, bd_ref,
                          kt_in, vt_in,
                          xo_ref, up_ref, q_ref, mq_ref, kt_ref, vt_ref, ktb_ref, vtb_ref, lft_ref, ct_ref, carry_ref,
                          *, tm, tiles_per_seq, n_heads):
    del kt_in, vt_in
    x = _ffn_value(x_ref[...], g1_ref, wgu_ref, wd_ref)
    xo_ref[...] = x
    xn, u_pool, q, mq, logf = _proj_rows(x, g_ref, w_ref, bf_ref, gq_ref, gmq_ref, bd_ref)
    up_ref[...] = u_pool
    q_ref[...] = q.astype(BF16)
    mq_ref[...] = mq.astype(BF16)

    kvt = _dot_nt(wkv_ref[...], xn)
    half = kvt.shape[0] // 2
    kt = _head_norm_t(kvt[:half], gk_ref[...])
    vt = kvt[half:]
    kt_ref[...] = kt
    vt_ref[...] = vt
    ktb_ref[...] = kt.astype(BF16)
    vtb_ref[...] = vt.astype(BF16)

    @pl.when(pl.program_id(0) % tiles_per_seq == 0)
    def _():
        carry_ref[...] = jnp.zeros_like(carry_ref)

    ft = logf.T[:n_heads]
    lft_ref[...] = ft
    lane = lax.broadcasted_iota(jnp.int32, ft.shape, 1)
    shift = 1
    while shift < tm:
        ft = ft + jnp.where(lane >= shift, pltpu.roll(ft, shift, axis=1), 0.0)
        shift *= 2
    ct = ft + jnp.concatenate([carry_ref[...]] * (tm // LANES), axis=1)
    carry_ref[...] = jnp.broadcast_to(ct[:, tm - 1:tm], carry_ref.shape)
    ct_ref[...] = ct


def _ffn_proj_prompt(x, ffn, wts, layer, n_seq, seq, n_heads, kv_bufs):
    t, d = x.shape
    tm = TOKEN_TILE
    tps = seq // tm
    depth, kv_rows, _ = wts["w_kv_t"].shape
    width = kv_rows // 2
    row = lambda w, dt: (pl.BlockSpec((tm, w), lambda i: (i, 0)), jax.ShapeDtypeStruct((t, w), dt))
    feat = lambda dt: (pl.BlockSpec((None, width, tm), lambda i: (i // tps, 0, i % tps)),
                       jax.ShapeDtypeStruct((n_seq, width, seq), dt))
    stacked = (pl.BlockSpec((None, None, width, tm), lambda i: (i // tps, layer, 0, i % tps)),
               jax.ShapeDtypeStruct((n_seq, depth, width, seq), F32))
    head_rows = (pl.BlockSpec((None, n_heads, tm), lambda i: (i // tps, 0, i % tps)),
                 jax.ShapeDtypeStruct((n_seq, n_heads, seq), F32))
    outs = [row(d, F32), row(POOL_WIDTH, F32), row(width, BF16), row(wts["g_mq"].shape[2], BF16),
            stacked, stacked, feat(BF16), feat(BF16), head_rows, head_rows]
    proj_names = ("g_mix", "w_rows", "w_kv_t", "b_f", "g_q", "g_mq", "g_k_col")
    in_specs = ([pl.BlockSpec((tm, d), lambda i: (i, 0))] + _ffn_specs(ffn, layer, True)
                + [_layer_block(wts[n].shape[1:], layer, n.startswith("w_")) for n in proj_names]
                + [_resident((NORM_TILE, NORM_TILE), lambda i: (0, 0)),
                   pl.BlockSpec(memory_space=pl.ANY), pl.BlockSpec(memory_space=pl.ANY)])
    n_in = len(in_specs)
    return pl.pallas_call(
        functools.partial(_ffn_proj_prompt_body, tm=tm, tiles_per_seq=tps, n_heads=n_heads),
        grid=(t // tm,),
        in_specs=in_specs,
        out_specs=[o[0] for o in outs],
        out_shape=[o[1] for o in outs],
        scratch_shapes=[pltpu.VMEM((n_heads, LANES), F32)],
        input_output_aliases={n_in - 2: 4, n_in - 1: 5},
        compiler_params=_params(1),
        name="ffn_proj_prompt",
    )(x, *ffn, *[wts[n] for n in proj_names], wts["bd"], *kv_bufs)


def _ffn_proj_sample_body(x_ref, g1_ref, wgu_ref, wd_ref, g_ref, w_ref, wkv_ref, bf_ref, gq_ref, gmq_ref, gk_ref, bd_ref,
                          xo_ref, up_ref, q_ref, mq_ref, k_ref, v_ref, lf_ref, *, n_heads):
    x = _ffn_value(x_ref[...], g1_ref, wgu_ref, wd_ref)
    xo_ref[...] = x
    xn, u_pool, q, mq, logf = _proj_rows(x, g_ref, w_ref, bf_ref, gq_ref, gmq_ref, bd_ref)
    up_ref[...] = u_pool
    q_ref[...] = q
    mq_ref[...] = mq
    lf_ref[...] = logf[:, :n_heads]
    kv = _dot(xn, wkv_ref[...])
    half = kv.shape[1] // 2
    k_ref[...] = _head_norm(kv[:, :half], gk_ref[...], bd_ref)
    v_ref[...] = kv[:, half:]


def _ffn_proj_sample(x, ffn, wts, layer, n_heads):
    t, d = x.shape
    width = wts["g_q"].shape[2]
    mem_w = wts["g_mq"].shape[2]
    row = lambda w: (pl.BlockSpec((t, w), lambda i: (0, 0)), jax.ShapeDtypeStruct((t, w), F32))
    outs = [row(d), row(POOL_WIDTH), row(width), row(mem_w), row(width), row(width), row(n_heads)]
    proj_names = ("g_mix", "w_rows", "w_kv", "b_f", "g_q", "g_mq", "g_k")
    return pl.pallas_call(
        functools.partial(_ffn_proj_sample_body, n_heads=n_heads),
        grid=(1,),
        in_specs=([pl.BlockSpec((t, d), lambda i: (0, 0))] + _ffn_specs(ffn, layer, False)
                  + [_layer_block(wts[n].shape[1:], layer) for n in proj_names]
                  + [pl.BlockSpec((NORM_TILE, NORM_TILE), lambda i: (0, 0))]),
        out_specs=[o[0] for o in outs],
        out_shape=[o[1] for o in outs],
        compiler_params=_params(1),
        name="ffn_proj_sample",
    )(x, *ffn, *[wts[n] for n in proj_names], wts["bd"])


def _fox_blocks(q_ref, kt_ref, vt_ref, ct_ref, o_ref, pair, blocks, blk):
    lane = lax.broadcasted_iota(jnp.int32, (blk, LANES), 1)
    causal = lax.broadcasted_iota(jnp.int32, (blk, blk), 0) >= lax.broadcasted_iota(jnp.int32, (blk, blk), 1)
    ck_rows = [ct_ref[pl.ds(2 * pair + e, 1), :] for e in range(2)]

    for i in blocks:
        n0, n1 = i * blk, (i + 1) * blk
        q2 = q_ref[n0:n1, :]
        zero = jnp.zeros_like(q2)
        outs = []
        for e in range(2):
            qe = jnp.where((lane >= HEAD_DIM) == bool(e), q2, zero)
            ck = ck_rows[e][:, :n1] - ck_rows[e][:, n0:n0 + 1]
            s_d = jnp.where(causal, _dot(qe, kt_ref[:, n0:n1]) - ck[:, n0:n1], NEG_BIG)
            m = jnp.max(s_d, axis=1, keepdims=True)
            if i:
                s_f = _dot(qe, kt_ref[:, :n0]) - ck[:, :n0]
                m = jnp.maximum(m, jnp.max(s_f, axis=1, keepdims=True))
            p_d = jnp.exp(s_d - m)
            den = jnp.sum(p_d, axis=1, keepdims=True)
            o = _dot_nt(p_d.astype(BF16), vt_ref[:, n0:n1])
            if i:
                p_f = jnp.exp(s_f - m)
                den = den + jnp.sum(p_f, axis=1, keepdims=True)
                o = o + _dot_nt(p_f.astype(BF16), vt_ref[:, :n0])
            outs.append(o / den)
        o_ref[n0:n1, :] = jnp.where(lane < HEAD_DIM, outs[0], outs[1]).astype(o_ref.dtype)


def _memkv_body(mem_ref, g_ref, w_ref, gk_ref, mk_ref, mv_ref):
    xn = _rms(mem_ref[...], g_ref[...]).astype(BF16)
    kvt = _dot_nt(w_ref[...], xn)
    half = kvt.shape[0] // 2
    mk_ref[...] = _head_norm_t(kvt[:half], gk_ref[...])
    mv_ref[...] = kvt[half:]


def _mem_kv(mem, g_mem, w_kv_t, g_k_col):
    n_seq, m_tok, d = mem.shape
    depth, two_w, _ = w_kv_t.shape
    w = two_w // 2
    out = (pl.BlockSpec((None, None, w, m_tok), lambda l, b: (b, l, 0, 0)),
           jax.ShapeDtypeStruct((n_seq, depth, w, m_tok), F32))
    return pl.pallas_call(
        _memkv_body,
        grid=(depth, n_seq),
        in_specs=[pl.BlockSpec((None, m_tok, d), lambda l, b: (b, 0, 0)),
                  pl.BlockSpec((None, 1, d), lambda l, b: (l, 0, 0)),
                  pl.BlockSpec((None, two_w, d), lambda l, b: (l, 0, 0)),
                  pl.BlockSpec((None, HEAD_DIM, 1), lambda l, b: (l, 0, 0))],
        out_specs=[out[0], out[0]],
        out_shape=[out[1], out[1]],
        compiler_params=_params(2),
        name="mem_kv",
    )(mem, g_mem, w_kv_t, g_k_col)


def _pool_window_lanes(shape, axis):
    lane = lax.broadcasted_iota(jnp.int32, shape, axis)
    w = jnp.full(shape, POOL_WINDOWS[-1], jnp.int32)
    for g in range(len(POOL_WINDOWS) - 2, -1, -1):
        w = jnp.where(lane < (g + 1) * POOL_GROUP, POOL_WINDOWS[g], w)
    return w


def _pair_attend(q2, kt2, vt2):
    lane = lax.broadcasted_iota(jnp.int32, q2.shape, 1)
    zero = jnp.zeros_like(q2)
    outs = []
    for e in range(2):
        qe = jnp.where((lane >= HEAD_DIM) == bool(e), q2, zero)
        s = _dot(qe, kt2)
        p = jnp.exp(s - jnp.max(s, axis=1, keepdims=True))
        outs.append(_dot_nt(p.astype(BF16), vt2) / jnp.sum(p, axis=1, keepdims=True))
    return jnp.where(lane < HEAD_DIM, outs[0], outs[1])


def _mix_out(x, o_pool, o_fox, o_mem, wo_ref):
    n_pool, n_fox = o_pool.shape[1], o_fox.shape[1]
    y = _dot(o_pool.astype(BF16), wo_ref[0:n_pool, :])
    y = y + _dot(o_fox.astype(BF16), wo_ref[n_pool:n_pool + n_fox, :])
    y = y + _dot(o_mem.astype(BF16), wo_ref[n_pool + n_fox:, :])
    return x + y


def _mix_ffn_prompt_tile(x_ref, up_ref, upp_ref, of_ref, mq_ref, mk_ref, mv_ref, wp_ref, ps_ref, wo_ref,
                         g2_ref, wgu_ref, wd_ref, o_ref, z_ref, *, tm, tiles_per_seq):
    tile = pl.program_id(0) % tiles_per_seq
    z = up_ref[...]
    z_ref[0:HALO, :] = jnp.where(tile == 0, 0.0, upp_ref[...])
    z_ref[HALO:, :] = z
    win = _pool_window_lanes(z.shape, 1)
    run = z
    wsum = jnp.zeros_like(z)
    for s in range(1, POOL_WINDOWS[-1]):
        run = run + z_ref[pl.ds(HALO - s, tm), :]
        if s + 1 in POOL_WINDOWS:
            wsum = jnp.where(win == s + 1, run, wsum)
    pos = tile * tm + lax.broadcasted_iota(jnp.int32, z.shape, 0)
    cnt = jnp.minimum(pos + 1, win).astype(F32)
    pooled = wsum / cnt - z
    o_pool = _dot(pooled.astype(BF16), wp_ref[...]) * ps_ref[...]

    o_mem = jnp.concatenate(
        [_pair_attend(mq_ref[:, o:o + LANES], mk_ref[o:o + LANES, :].astype(BF16), mv_ref[o:o + LANES, :].astype(BF16))
         for o in range(0, mq_ref.shape[1], LANES)], axis=1)

    x = _mix_out(x_ref[...], o_pool, of_ref[...], o_mem, wo_ref)
    o_ref[...] = _ffn_value(x, g2_ref, wgu_ref, wd_ref)


def _mix_ffn_prompt(x, u_pool, o_fox, mq, mem_kt, mem_vt, ffn, wts, layer, seq):
    t, d = x.shape
    tm = TOKEN_TILE
    tps = seq // tm
    halo_blocks = tm // HALO
    mem_spec = pl.BlockSpec((None, None) + mem_kt.shape[2:], lambda i: (i // tps, layer, 0, 0))
    return pl.pallas_call(
        functools.partial(_mix_ffn_prompt_tile, tm=tm, tiles_per_seq=tps),
        grid=(t // tm,),
        in_specs=[
            pl.BlockSpec((tm, d), lambda i: (i, 0)),
            pl.BlockSpec((tm, POOL_WIDTH), lambda i: (i, 0)),
            pl.BlockSpec((HALO, POOL_WIDTH), lambda i: (jnp.maximum(i * halo_blocks - 1, 0), 0)),
            pl.BlockSpec((tm, o_fox.shape[1]), lambda i: (i, 0)),
            pl.BlockSpec((tm, mq.shape[1]), lambda i: (i, 0)),
            mem_spec, mem_spec,
            _layer_block(wts["w_pool"].shape[1:], layer, True),
            _layer_block(wts["pool_scale"].shape[1:], layer),
            _layer_block(wts["w_out"].shape[1:], layer, True),
        ] + _ffn_specs(ffn, layer, True),
        out_specs=pl.BlockSpec((tm, d), lambda i: (i, 0)),
        out_shape=jax.ShapeDtypeStruct((t, d), F32),
        scratch_shapes=[pltpu.VMEM((tm + HALO, POOL_WIDTH), F32)],
        compiler_params=_params(1),
        name="mix_ffn_prompt",
    )(x, u_pool, u_pool, o_fox, mq, mem_kt, mem_vt, wts["w_pool"], wts["pool_scale"], wts["w_out"], *ffn)


def _column_attend(q_rows, kt, vt, bias, k_new_t, v_new_t, n_heads):
    q_t = q_rows.T
    lane = lax.broadcasted_iota(jnp.int32, q_t.shape, 1)
    out = jnp.zeros(q_t.shape, F32)
    for h in range(n_heads):
        s = _dot(q_rows, kt(h))[h:h + 1, :]
        if bias is not None:
            s = s + bias[h:h + 1, :]
        m = jnp.max(s, axis=1, keepdims=True)
        if k_new_t is not None:
            s_new = jnp.sum(q_t[:, h:h + 1] * k_new_t[:, h:h + 1], axis=0, keepdims=True)
            m = jnp.maximum(m, s_new)
        p = jnp.exp(s - m)
        den = jnp.sum(p, axis=1, keepdims=True)
        o = jnp.sum(vt(h) * p, axis=1, keepdims=True)
        if k_new_t is not None:
            p_new = jnp.exp(s_new - m)
            den = den + p_new
            o = o + p_new * v_new_t[:, h:h + 1]
        out = jnp.where(lane == h, o / den, out)
    return out


def _decode_group(pt_ref, q_ref, ks_ref, vs_ref, lfs_ref, mq_ref, ck_hbm, cv_hbm, cf_hbm, mk_ref, mv_ref,
                  of_ref, om_ref, kbuf, vbuf, fbuf, sem, *, layer, n_pages, page, n_heads, n_mem_heads, group):
    step = pl.program_id(0)
    slot = step % 2

    def page_copies(at_step, sl):
        cps = []
        for u in range(group):
            for j in range(n_pages):
                pg = pt_ref[at_step * group + u, j]
                cols = pl.ds(j * page, page)
                cps.append(pltpu.make_async_copy(ck_hbm.at[pg, layer], kbuf.at[sl, u, :, :, cols], sem.at[sl, 0]))
                cps.append(pltpu.make_async_copy(cv_hbm.at[pg, layer], vbuf.at[sl, u, :, :, cols], sem.at[sl, 1]))
                cps.append(pltpu.make_async_copy(cf_hbm.at[pg, layer], fbuf.at[sl, u, :, cols], sem.at[sl, 2]))
        return cps

    @pl.when(step == 0)
    def _():
        for cp in page_copies(0, 0):
            cp.start()

    @pl.when(step + 1 < pl.num_programs(0))
    def _():
        for cp in page_copies(step + 1, 1 - slot):
            cp.start()

    for cp in page_copies(step, slot):
        cp.wait()

    past = n_pages * page
    for u in range(group):
        ft = fbuf[slot, u]
        lane = lax.broadcasted_iota(jnp.int32, ft.shape, 1)
        suffix = ft
        shift = 1
        while shift < past:
            suffix = suffix + jnp.where(lane + shift < past, pltpu.roll(suffix, past - shift, axis=1), 0.0)
            shift *= 2
        bias = (suffix - ft) + lfs_ref[u]
        o_t = _column_attend(q_ref[u], lambda h: kbuf[slot, u, h], lambda h: vbuf[slot, u, h], bias,
                             ks_ref[u].T, vs_ref[u].T, n_heads)
        of_ref[u] = o_t.T
        om_t = _column_attend(mq_ref[u], lambda h: mk_ref[u, h], lambda h: mv_ref[u, h], None, None, None,
                              n_mem_heads)
        om_ref[u] = om_t.T


def _fox_decode_body(pt_ref, q_ref, kt_ref, vt_ref, ct_ref,
                     dq_ref, dks_ref, dvs_ref, dlf_ref, dmq_ref, ck_hbm, cv_hbm, cf_hbm, dmk_ref, dmv_ref,
                     o_ref, dof_ref, dom_ref, kbuf, vbuf, fbuf, sem, *, blk, splits, decode_kw):
    _decode_group(pt_ref, dq_ref, dks_ref, dvs_ref, dlf_ref, dmq_ref, ck_hbm, cv_hbm, cf_hbm, dmk_ref, dmv_ref,
                  dof_ref, dom_ref, kbuf, vbuf, fbuf, sem, **decode_kw)
    part = pl.program_id(0) % len(splits)
    pair = (pl.program_id(0) // len(splits)) % (FOX_PAIRS)
    for k, blocks in enumerate(splits):
        @pl.when(part == k)
        def _(blocks=blocks):
            _fox_blocks(q_ref, kt_ref, vt_ref, ct_ref, o_ref, pair, blocks, blk)


def _fox_decode(q, ktb, vtb, ct, n_heads,
                page_table, q_s, k_s, v_s, lf_s, mq_s, cache_kt, cache_vt, cache_ft, cache_mem_kt, cache_mem_vt, layer):
    n_seq, seq, width = q.shape
    assert width // LANES == FOX_PAIRS
    blk = ATTN_BLOCK
    n_parts = len(FOX_SPLITS)
    n_steps = n_seq * FOX_PAIRS * n_parts
    n_samples, n_pages = page_table.shape
    _, _, _, hd, page = cache_kt.shape
    _, _, n_mem_heads, _, m_tok = cache_mem_kt.shape
    past = n_pages * page
    group = n_samples // n_steps
    assert group * n_steps == n_samples and FOX_SPLITS[-1][-1] + 1 == seq // blk
    seq_of = lambda s: s // (FOX_PAIRS * n_parts)
    pair_of = lambda s: (s // n_parts) % FOX_PAIRS
    rows = pl.BlockSpec((None, seq, LANES), lambda s, pt: (seq_of(s), 0, pair_of(s)))
    feats = pl.BlockSpec((None, LANES, seq), lambda s, pt: (seq_of(s), pair_of(s), 0))
    per_group = lambda r, w: pl.BlockSpec((group, r, w), lambda s, pt: (s, 0, 0))
    cache_mem_spec = pl.BlockSpec((group, None, n_mem_heads, hd, m_tok), lambda s, pt: (s, layer, 0, 0, 0))
    any_spec = pl.BlockSpec(memory_space=pl.ANY)
    grid_spec = pltpu.PrefetchScalarGridSpec(
        num_scalar_prefetch=1,
        grid=(n_steps,),
        in_specs=[rows, feats, feats, pl.BlockSpec((None, n_heads, seq), lambda s, pt: (seq_of(s), 0, 0)),
                  per_group(n_heads, hd), per_group(n_heads, hd), per_group(n_heads, hd), per_group(n_heads, 1),
                  per_group(SUBLANES, hd), any_spec, any_spec, any_spec, cache_mem_spec, cache_mem_spec],
        out_specs=[rows, per_group(n_heads, hd), per_group(SUBLANES, hd)],
        scratch_shapes=[pltpu.VMEM((2, group, n_heads, hd, past), F32),
                        pltpu.VMEM((2, group, n_heads, hd, past), F32),
                        pltpu.VMEM((2, group, n_heads, past), F32),
                        pltpu.SemaphoreType.DMA((2, 3))],
    )
    return pl.pallas_call(
        functools.partial(
            _fox_decode_body, blk=blk, splits=FOX_SPLITS,
            decode_kw=dict(layer=layer, n_pages=n_pages, page=page, n_heads=n_heads, n_mem_heads=n_mem_heads,
                           group=group)),
        grid_spec=grid_spec,
        out_shape=[jax.ShapeDtypeStruct((n_seq, seq, width), BF16),
                   jax.ShapeDtypeStruct((n_samples, n_heads, hd), F32),
                   jax.ShapeDtypeStruct((n_samples, SUBLANES, hd), F32)],
        compiler_params=_params(1),
        name="fox_decode",
    )(page_table, q, ktb, vtb, ct, q_s, k_s, v_s, lf_s, mq_s, cache_kt, cache_vt, cache_ft, cache_mem_kt, cache_mem_vt)


def _mix_ffn_sample_body(x_ref, u_ref, st_ref, of_ref, om_ref, wp_ref, ps_ref, wo_ref, g2_ref, wgu_ref, wd_ref, o_ref):
    u = u_ref[...]
    st = st_ref[...]
    row = lax.broadcasted_iota(jnp.int32, st.shape, 1)
    win3 = _pool_window_lanes(st.shape, 2)
    hist = jnp.sum(jnp.where(row >= POOL_HIST + 1 - win3, st, 0.0), axis=1)
    win = _pool_window_lanes(u.shape, 1).astype(F32)
    pooled = (hist + u) / win - u
    o_pool = _dot(pooled.astype(BF16), wp_ref[...]) * ps_ref[...]
    x = _mix_out(x_ref[...], o_pool, of_ref[...], om_ref[...], wo_ref)
    o_ref[...] = _ffn_value(x, g2_ref, wgu_ref, wd_ref)


def _mix_ffn_sample(x, u_pool, state_pool, o_fox, o_mem, ffn, wts, layer):
    t, d = x.shape
    full = lambda a: pl.BlockSpec(a.shape, lambda i: (0,) * a.ndim)
    return pl.pallas_call(
        _mix_ffn_sample_body,
        grid=(1,),
        in_specs=[
            full(x), full(u_pool),
            pl.BlockSpec((t, None) + state_pool.shape[2:], lambda i: (0, layer, 0, 0)),
            full(o_fox), full(o_mem),
            _layer_block(wts["w_pool"].shape[1:], layer),
            _layer_block(wts["pool_scale"].shape[1:], layer),
            _layer_block(wts["w_out"].shape[1:], layer),
        ] + _ffn_specs(ffn, layer, False),
        out_specs=full(x),
        out_shape=jax.ShapeDtypeStruct((t, d), F32),
        compiler_params=_params(1),
        name="mix_ffn_sample",
    )(x, u_pool, state_pool, o_fox, o_mem, wts["w_pool"], wts["pool_scale"], wts["w_out"], *ffn)


def _block_diag(blocks):
    n, r, c = blocks.shape
    eye = jnp.eye(n, dtype=blocks.dtype)
    return (eye[:, None, :, None] * blocks[:, :, None, :]).reshape(n * r, n * c)


def kernel(x_prompt, x_sample, cache_k, cache_v, cache_logf, state_pool, cache_mem_k, cache_mem_v, page_table, mem_prompt, g_ffn1, w_ffn1_gu, w_ffn1_down, g_mix, w_in, b_forget, g_fox_q, g_fox_k, w_pool, pool_scale, g_mem, w_mem_kv, g_mem_q, g_mem_k, w_out, g_ffn2, w_ffn2_gu, w_ffn2_down):
    n_seq, seq, d = x_prompt.shape
    n_samples = x_sample.shape[0]
    depth = w_in.shape[0]
    n_heads = cache_k.shape[3]
    n_mem_heads = cache_mem_k.shape[3]
    fox_w = n_heads * HEAD_DIM
    mem_w = n_mem_heads * HEAD_DIM
    assert x_sample.shape[1] == 1 and seq % TOKEN_TILE == 0 and seq % ATTN_BLOCK == 0

    o_k = POOL_WIDTH + fox_w
    o_f = POOL_WIDTH + 3 * fox_w
    w_rows = jnp.concatenate(
        [w_in[:, :, :o_k], w_in[:, :, o_f + n_heads:], w_in[:, :, o_f:o_f + n_heads],
         jnp.zeros((depth, d, LANES - n_heads), w_in.dtype)], axis=2).astype(BF16)
    w_kv = w_in[:, :, o_k:o_f].astype(BF16)
    row3 = lambda a: a.reshape(depth, 1, -1)
    col3 = lambda a: a.reshape(depth, -1, 1)
    wts = {
        "g_mix": row3(g_mix),
        "w_rows": w_rows,
        "w_kv": w_kv,
        "w_kv_t": jnp.swapaxes(w_kv, 1, 2),
        "b_f": jnp.pad(b_forget, ((0, 0), (0, LANES - n_heads))).reshape(depth, 1, LANES),
        "g_q": row3(jnp.tile(g_fox_q, (1, n_heads))),
        "g_k": row3(jnp.tile(g_fox_k, (1, n_heads))),
        "g_k_col": col3(g_fox_k),
        "g_mq": row3(jnp.tile(g_mem_q, (1, n_mem_heads))),
        "bd": _block_diag(jnp.full((NORM_TILE // HEAD_DIM, HEAD_DIM, HEAD_DIM), 1.0 / HEAD_DIM, BF16)),
        "w_pool": jax.vmap(_block_diag)(w_pool).astype(BF16),
        "pool_scale": row3(pool_scale),
        "w_out": w_out.astype(BF16),
    }
    ffn1 = (row3(g_ffn1), w_ffn1_gu.astype(BF16), w_ffn1_down.astype(BF16))
    ffn2 = (row3(g_ffn2), w_ffn2_gu.astype(BF16), w_ffn2_down.astype(BF16))

    cache_kt = jnp.transpose(cache_k, (0, 1, 3, 4, 2))
    cache_vt = jnp.transpose(cache_v, (0, 1, 3, 4, 2))
    cache_ft = jnp.transpose(cache_logf, (0, 1, 3, 2))
    cache_mem_kt = jnp.transpose(cache_mem_k, (0, 1, 3, 4, 2))
    cache_mem_vt = jnp.transpose(cache_mem_v, (0, 1, 3, 4, 2))

    mem_kt, mem_vt = _mem_kv(mem_prompt, row3(g_mem), jnp.swapaxes(w_mem_kv, 1, 2).astype(BF16), col3(g_mem_k))

    xp = x_prompt.reshape(n_seq * seq, d)
    xs = x_sample.reshape(n_samples, d)
    kv_bufs = (jnp.zeros((n_seq, depth, fox_w, seq), F32), jnp.zeros((n_seq, depth, fox_w, seq), F32))
    fp_l, pp_l, ks_l, vs_l, fs_l, ps_l = [], [], [], [], [], []
    for layer in range(depth):
        xp, u_pool, q, mq, kt_all, vt_all, ktb, vtb, lft, ct = _ffn_proj_prompt(
            xp, ffn1, wts, layer, n_seq, seq, n_heads, kv_bufs)
        kv_bufs = (kt_all, vt_all)
        fp_l.append(lft)
        pp_l.append(u_pool.reshape(n_seq, seq, POOL_WIDTH)[:, seq - POOL_HIST:])

        xs, u_s, q_s, mq_s, k_s, v_s, lf_s = _ffn_proj_sample(xs, ffn1, wts, layer, n_heads)
        heads = lambda a, h: a.reshape(n_samples, h, HEAD_DIM)
        mq_pad = jnp.pad(heads(mq_s, n_mem_heads), ((0, 0), (0, SUBLANES - n_mem_heads), (0, 0)))
        o_fox, o_fox_s, o_mem_s = _fox_decode(
            q.reshape(n_seq, seq, fox_w), ktb, vtb, ct, n_heads,
            page_table, heads(q_s, n_heads), heads(k_s, n_heads), heads(v_s, n_heads),
            lf_s.reshape(n_samples, n_heads, 1), mq_pad, cache_kt, cache_vt, cache_ft, cache_mem_kt, cache_mem_vt,
            layer)
        xp = _mix_ffn_prompt(xp, u_pool, o_fox.reshape(n_seq * seq, fox_w), mq, mem_kt, mem_vt, ffn2, wts, layer, seq)
        xs = _mix_ffn_sample(xs, u_s, state_pool, o_fox_s.reshape(n_samples, fox_w),
                             o_mem_s[:, :n_mem_heads].reshape(n_samples, mem_w), ffn2, wts, layer)
        ks_l.append(k_s.reshape(n_samples, 1, n_heads, HEAD_DIM))
        vs_l.append(v_s.reshape(n_samples, 1, n_heads, HEAD_DIM))
        fs_l.append(lf_s.reshape(n_samples, 1, n_heads))
        ps_l.append(jnp.concatenate([state_pool[:, layer, 1:], u_s[:, None, :]], axis=1))

    stack = lambda xs_: jnp.stack(xs_, axis=1)
    to_token_major = lambda a, h: jnp.transpose(a.reshape(a.shape[0], depth, h, HEAD_DIM, a.shape[3]), (0, 1, 4, 2, 3))
    return (xp.reshape(n_seq, seq, d), xs.reshape(n_samples, 1, d),
            to_token_major(kv_bufs[0], n_heads), to_token_major(kv_bufs[1], n_heads),
            jnp.transpose(stack(fp_l), (0, 1, 3, 2)), stack(pp_l),
            to_token_major(mem_kt, n_mem_heads), to_token_major(mem_vt, n_mem_heads),
            stack(ks_l), stack(vs_l), stack(fs_l), stack(ps_l))
```
